```python
import math
import jax, jax.numpy as jnp
from jax import lax
import numpy as np

D_MODEL = 1024
BATCH = 8
SEQ = 4096
DEPTH = 4

N_HYB = (DEPTH + 1) // 2
N_REC = DEPTH // 2
NORM_EPS = 1e-6
CONV_K = 4
D_FF = 2816
D_SSM = D_MODEL
SSM_HEAD_DIM = 64
SSM_HEADS = D_SSM // SSM_HEAD_DIM
SSM_GROUPS = 2
SSM_STATE = 128
SSM_CONV_DIM = D_SSM + 2 * SSM_GROUPS * SSM_STATE
SSD_CHUNK = 128
FOX_HEAD_DIM = 128
FOX_HEADS = 8
D_FOX = FOX_HEADS * FOX_HEAD_DIM
Q_BLOCK = 128
HYB_IN = D_SSM + SSM_CONV_DIM + SSM_HEADS + 3 * D_FOX + FOX_HEADS
D_RNN = D_MODEL
RNN_BLOCKS = 8
RNN_BLOCK = D_RNN // RNN_BLOCKS
RG_LRU_C = 8.0

kernel_name = "hybrid_ssd_fox_rglru_macaron_sandwich"


def rms_norm(x, g):
    xf = x.astype(jnp.float32)
    y = xf * lax.rsqrt(jnp.mean(xf * xf, axis=-1, keepdims=True) + NORM_EPS)
    return (y * g.astype(jnp.float32)).astype(x.dtype)


def swiglu(x, w_in, w_out):
    gate, up = jnp.split(x @ w_in, 2, axis=-1)
    return (jax.nn.silu(gate) * up) @ w_out


def causal_conv(x, w, bias):
    s = x.shape[1]
    kw = w.shape[0]
    xp = jnp.pad(x, ((0, 0), (kw - 1, 0), (0, 0)))
    y = xp[:, 0:s] * w[0]
    for k in range(1, kw):
        y = y + xp[:, k:k + s] * w[k]
    return y + bias


def ssd_chunked(xh, dt, a, bm, cm):
    b, s, nh, p = xh.shape
    g, n = bm.shape[-2:]
    kh = nh // g
    c = s // SSD_CHUNK
    X = (xh.astype(jnp.float32) * dt[..., None]).reshape(b, c, SSD_CHUNK, g, kh, p)
    Bc = bm.astype(jnp.float32).reshape(b, c, SSD_CHUNK, g, n)
    Cc = cm.astype(jnp.float32).reshape(b, c, SSD_CHUNK, g, n)
    dA = (dt * a).reshape(b, c, SSD_CHUNK, g, kh).transpose(0, 3, 4, 1, 2)
    a_cs = jnp.cumsum(dA, axis=-1)
    causal = jnp.tril(jnp.ones((SSD_CHUNK, SSD_CHUNK), dtype=bool))
    seg = a_cs[..., :, None] - a_cs[..., None, :]
    L = jnp.exp(jnp.where(causal, seg, -jnp.inf))
    cb = jnp.einsum("bclgn,bcsgn->bgcls", Cc, Bc)
    y_diag = jnp.einsum("bgkcls,bcsgkp->bclgkp", L * cb[:, :, None], X)
    decay_states = jnp.exp(a_cs[..., -1:] - a_cs)
    states = jnp.einsum("bclgn,bgkcl,bclgkp->bcgkpn", Bc, decay_states, X)
    chunk_tot = a_cs[..., -1]
    chunk_cs = jnp.cumsum(chunk_tot, axis=-1)
    cs_prev = chunk_cs - chunk_tot
    seg_c = cs_prev[..., :, None] - chunk_cs[..., None, :]
    strict = jnp.tril(jnp.ones((c, c), dtype=bool), k=-1)
    decay_chunk = jnp.exp(jnp.where(strict, seg_c, -jnp.inf))
    init_states = jnp.einsum("bgkzc,bcgkpn->bzgkpn", decay_chunk, states)
    state_decay_out = jnp.exp(a_cs)
    y_off = jnp.einsum("bclgn,bcgkpn,bgkcl->bclgkp", Cc, init_states, state_decay_out)
    return (y_diag + y_off).reshape(b, s, nh, p)


def forgetting_attention(q, k, v, log_f):
    b, s, h, d = q.shape
    nblk = s // Q_BLOCK
    cum = jnp.cumsum(log_f, axis=1).transpose(0, 2, 1)
    qb = q.reshape(b, nblk, Q_BLOCK, h, d).transpose(1, 0, 2, 3, 4)
    cqb = cum.reshape(b, h, nblk, Q_BLOCK).transpose(2, 0, 1, 3)
    kpos = jnp.arange(s)
    scale = d ** -0.5

    def block(args):
        qi, ci, i = args
        logits = jnp.einsum("bqhd,bkhd->bhqk", qi, k).astype(jnp.float32) * scale
        logits = logits + ci[..., :, None] - cum[:, :, None, :]
        qpos = i * Q_BLOCK + jnp.arange(Q_BLOCK)
        mask = kpos[None, :] <= qpos[:, None]
        probs = jax.nn.softmax(jnp.where(mask, logits, -jnp.inf), axis=-1)
        return jnp.einsum("bhqk,bkhd->bqhd", probs.astype(v.dtype), v)

    out = lax.map(block, (qb, cqb, jnp.arange(nblk)))
    return out.transpose(1, 0, 2, 3, 4).reshape(b, s, h, d)


def hybrid_mixer(h, w_in, conv_w, conv_b, dt_bias, a_log, d_skip, ssm_norm_g, fox_b_f, w_out):
    b, s, _ = h.shape
    sizes = (D_SSM, SSM_CONV_DIM, SSM_HEADS, D_FOX, D_FOX, D_FOX)
    offs = np.cumsum(sizes).tolist()
    z, xbc, dt_raw, q, k, v, f_raw = jnp.split(h @ w_in, offs, axis=-1)
    xbc = jax.nn.silu(causal_conv(xbc, conv_w, conv_b))
    xs, bm, cm = jnp.split(xbc, [D_SSM, D_SSM + SSM_GROUPS * SSM_STATE], axis=-1)
    xh = xs.reshape(b, s, SSM_HEADS, SSM_HEAD_DIM)
    dt = jax.nn.softplus(dt_raw.astype(jnp.float32) + dt_bias.astype(jnp.float32))
    a = -jnp.exp(a_log.astype(jnp.float32))
    y = ssd_chunked(xh, dt, a,
                    bm.reshape(b, s, SSM_GROUPS, SSM_STATE),
                    cm.reshape(b, s, SSM_GROUPS, SSM_STATE))
    y = y + d_skip.astype(jnp.float32)[:, None] * xh.astype(jnp.float32)
    y = y.reshape(b, s, D_SSM) * jax.nn.silu(z.astype(jnp.float32))
    yg = y.reshape(b, s, SSM_GROUPS, D_SSM // SSM_GROUPS)
    yg = yg * lax.rsqrt(jnp.mean(yg * yg, axis=-1, keepdims=True) + NORM_EPS)
    y_ssm = (yg.reshape(b, s, D_SSM) * ssm_norm_g.astype(jnp.float32)).astype(h.dtype)
    log_f = jax.nn.log_sigmoid(f_raw.astype(jnp.float32) + fox_b_f.astype(jnp.float32))
    o = forgetting_attention(q.reshape(b, s, FOX_HEADS, FOX_HEAD_DIM),
                             k.reshape(b, s, FOX_HEADS, FOX_HEAD_DIM),
                             v.reshape(b, s, FOX_HEADS, FOX_HEAD_DIM), log_f)
    o = o.reshape(b, s, D_FOX).astype(h.dtype)
    return jnp.concatenate([y_ssm, o], axis=-1) @ w_out


def rg_lru(x, w_a, b_a, w_x, b_x, lam):
    b, s, d = x.shape
    xb = x.reshape(b, s, RNN_BLOCKS, RNN_BLOCK)
    r = jax.nn.sigmoid((jnp.einsum("bsnj,njk->bsnk", xb, w_a).reshape(b, s, d) + b_a).astype(jnp.float32))
    i = jax.nn.sigmoid((jnp.einsum("bsnj,njk->bsnk", xb, w_x).reshape(b, s, d) + b_x).astype(jnp.float32))
    log_a = RG_LRU_C * r * jax.nn.log_sigmoid(lam.astype(jnp.float32))
    a = jnp.exp(log_a)
    u = jnp.sqrt(-jnp.expm1(2.0 * log_a)) * (i * x.astype(jnp.float32))

    def combine(left, right):
        a1, b1 = left
        a2, b2 = right
        return a1 * a2, a2 * b1 + b2

    _, hs = lax.associative_scan(combine, (a, u), axis=1)
    return hs.astype(x.dtype)


def recurrent_mixer(h, w_in, conv_w, conv_b, w_a, b_a, w_x, b_x, lam, w_out):
    gate, xr = jnp.split(h @ w_in, 2, axis=-1)
    gate = jax.nn.gelu(gate, approximate=True)
    xr = causal_conv(xr, conv_w, conv_b)
    return (rg_lru(xr, w_a, b_a, w_x, b_x, lam) * gate) @ w_out


def setup_inputs(seed: int = 0) -> dict:
    key = jax.random.key(seed)
    ks = jax.random.split(key, 24)
    f32 = jnp.float32

    def nrm(k, shape, scale):
        return jax.random.normal(k, shape, f32) * scale

    def unif(k, shape, lo, hi):
        return jax.random.uniform(k, shape, f32, lo, hi)

    x = nrm(ks[0], (BATCH, SEQ, D_MODEL), 1.0)
    norm_g = 1.0 + nrm(ks[1], (DEPTH, 6, D_MODEL), 0.05)
    ffn_w_in = nrm(ks[2], (DEPTH, 2, D_MODEL, 2 * D_FF), D_MODEL ** -0.5)
    ffn_w_out = nrm(ks[3], (DEPTH, 2, D_FF, D_MODEL), D_FF ** -0.5)
    hyb_w_in = nrm(ks[4], (N_HYB, D_MODEL, HYB_IN), D_MODEL ** -0.5)
    ssm_conv_w = nrm(ks[5], (N_HYB, CONV_K, SSM_CONV_DIM), CONV_K ** -0.5)
    ssm_conv_b = nrm(ks[6], (N_HYB, SSM_CONV_DIM), 0.02)
    dt0 = jnp.exp(unif(ks[7], (N_HYB, SSM_HEADS), math.log(1e-3), math.log(1e-1)))
    ssm_dt_bias = dt0 + jnp.log(-jnp.expm1(-dt0))
    ssm_a_log = jnp.log(unif(ks[8], (N_HYB, SSM_HEADS), 1.0, 16.0))
    ssm_d = 1.0 + nrm(ks[9], (N_HYB, SSM_HEADS), 0.1)
    ssm_norm_g = 1.0 + nrm(ks[10], (N_HYB, D_SSM), 0.05)
    fox_b_f = unif(ks[11], (N_HYB, FOX_HEADS), 1.0, 4.0)
    hyb_w_out = nrm(ks[12], (N_HYB, D_SSM + D_FOX, D_MODEL), (D_SSM + D_FOX) ** -0.5)
    rec_w_in = nrm(ks[13], (N_REC, D_MODEL, 2 * D_RNN), D_MODEL ** -0.5)
    rec_conv_w = nrm(ks[14], (N_REC, CONV_K, D_RNN), CONV_K ** -0.5)
    rec_conv_b = nrm(ks[15], (N_REC, D_RNN), 0.02)
    rec_w_a = nrm(ks[16], (N_REC, RNN_BLOCKS, RNN_BLOCK, RNN_BLOCK), RNN_BLOCK ** -0.5)
    rec_b_a = nrm(ks[17], (N_REC, D_RNN), 0.02)
    rec_w_x = nrm(ks[18], (N_REC, RNN_BLOCKS, RNN_BLOCK, RNN_BLOCK), RNN_BLOCK ** -0.5)
    rec_b_x = nrm(ks[19], (N_REC, D_RNN), 0.02)
    a0 = unif(ks[20], (N_REC, D_RNN), 0.9, 0.999)
    s0 = a0 ** (1.0 / RG_LRU_C)
    rec_lambda = jnp.log(s0) - jnp.log1p(-s0)
    rec_w_out = nrm(ks[21], (N_REC, D_RNN, D_MODEL), D_RNN ** -0.5)
    return {"x": x, "norm_g": norm_g, "ffn_w_in": ffn_w_in, "ffn_w_out": ffn_w_out,
            "hyb_w_in": hyb_w_in, "ssm_conv_w": ssm_conv_w, "ssm_conv_b": ssm_conv_b,
            "ssm_dt_bias": ssm_dt_bias, "ssm_a_log": ssm_a_log, "ssm_d": ssm_d,
            "ssm_norm_g": ssm_norm_g, "fox_b_f": fox_b_f, "hyb_w_out": hyb_w_out,
            "rec_w_in": rec_w_in, "rec_conv_w": rec_conv_w, "rec_conv_b": rec_conv_b,
            "rec_w_a": rec_w_a, "rec_b_a": rec_b_a, "rec_w_x": rec_w_x, "rec_b_x": rec_b_x,
            "rec_lambda": rec_lambda, "rec_w_out": rec_w_out}


def reference(x, norm_g, ffn_w_in, ffn_w_out, hyb_w_in, ssm_conv_w, ssm_conv_b, ssm_dt_bias,
              ssm_a_log, ssm_d, ssm_norm_g, fox_b_f, hyb_w_out, rec_w_in, rec_conv_w, rec_conv_b,
              rec_w_a, rec_b_a, rec_w_x, rec_b_x, rec_lambda, rec_w_out):
    for layer in range(DEPTH):
        g = norm_g[layer]
        h = swiglu(rms_norm(x, g[0]), ffn_w_in[layer, 0], ffn_w_out[layer, 0])
        x = x + 0.5 * rms_norm(h, g[1])
        h = rms_norm(x, g[2])
        if layer % 2 == 0:
            i = layer // 2
            h = hybrid_mixer(h, hyb_w_in[i], ssm_conv_w[i], ssm_conv_b[i], ssm_dt_bias[i],
                             ssm_a_log[i], ssm_d[i], ssm_norm_g[i], fox_b_f[i], hyb_w_out[i])
        else:
            j = layer // 2
            h = recurrent_mixer(h, rec_w_in[j], rec_conv_w[j], rec_conv_b[j], rec_w_a[j], rec_b_a[j],
                                rec_w_x[j], rec_b_x[j], rec_lambda[j], rec_w_out[j])
        x = x + rms_norm(h, g[3])
        h = swiglu(rms_norm(x, g[4]), ffn_w_in[layer, 1], ffn_w_out[layer, 1])
        x = x + 0.5 * rms_norm(h, g[5])
    return x
```

```python
import functools

import jax
import jax.numpy as jnp
from jax import lax
from jax.experimental import pallas as pl
from jax.experimental.pallas import tpu as pltpu

F32 = jnp.float32
BF16 = jnp.bfloat16

NORM_EPS = 1e-6
CONV_K = 4
SSM_HEAD_DIM = 64
SSM_GROUPS = 2
SSM_STATE = 128
SSD_CHUNK = 128
FOX_HEAD_DIM = 128
RG_LRU_C = 8.0
RNN_BLOCK = 128

V7X_LANES = 128
V7X_SUBLANES = 8
VMEM_LIMIT_BYTES = 56 * 1024 * 1024

TOKEN_TILE = 512
FFN_CHUNK = 256
FOX_BLOCK = 256
REC_TILE = 512


def _params(*semantics):
    return pltpu.CompilerParams(dimension_semantics=semantics,
                                vmem_limit_bytes=VMEM_LIMIT_BYTES)


def _const_spec(shape):
    zeros = (0,) * len(shape)
    return pl.BlockSpec(shape, lambda *_: zeros)


def _rms(x, g):
    return x * lax.rsqrt(jnp.mean(x * x, axis=-1, keepdims=True) + NORM_EPS) * g


def _silu(x):
    return x * jax.nn.sigmoid(x)


def _softplus(x):
    return jnp.maximum(x, 0.0) + jnp.log1p(jnp.exp(-jnp.abs(x)))


def _dot(a, b):
    return jnp.dot(a, b, preferred_element_type=F32)


def _split3(x):
    hi = x.astype(BF16)
    r = x - hi.astype(F32)
    mid = r.astype(BF16)
    lo = (r - mid.astype(F32)).astype(BF16)
    return hi, mid, lo


def _dot_exact_lhs(m, x):
    hi, mid, lo = _split3(x)
    return _dot(m, hi) + _dot(m, mid) + _dot(m, lo)


def _dot_exact_rhs(x, m):
    hi, mid, lo = _split3(x)
    return _dot(hi, m) + _dot(mid, m) + _dot(lo, m)


def _ffn_body(x_ref, gpre_ref, gpost_ref, win_ref, wout_ref, o_ref, h_ref, *, fc, n_chunks):
    x = x_ref[...]
    xn = _rms(x, gpre_ref[...]).astype(BF16)
    for c in range(n_chunks):
        gu = _dot(xn, win_ref[:, 2 * fc * c:2 * fc * (c + 1)])
        h_ref[:, fc * c:fc * (c + 1)] = (_silu(gu[:, :fc]) * gu[:, fc:]).astype(BF16)
    y = _dot(h_ref[...], wout_ref[...])
    o_ref[...] = x + 0.5 * _rms(y, gpost_ref[...])


def _ffn(xt, g_pre, g_post, w_in, w_out):
    t, d = xt.shape
    dff = w_out.shape[0]
    fc = FFN_CHUNK
    n_chunks = dff // fc
    assert n_chunks * fc == dff
    tm = min(TOKEN_TILE, t)
    w_perm = jnp.stack([w_in[:, :dff].reshape(d, n_chunks, fc),
                        w_in[:, dff:].reshape(d, n_chunks, fc)], axis=2)
    w_perm = w_perm.reshape(d, 2 * dff).astype(BF16)
    return pl.pallas_call(
        functools.partial(_ffn_body, fc=fc, n_chunks=n_chunks),
        grid=(t // tm,),
        in_specs=[pl.BlockSpec((tm, d), lambda i: (i, 0)),
                  _const_spec((1, d)), _const_spec((1, d)),
                  _const_spec((d, 2 * dff)), _const_spec((dff, d))],
        out_specs=pl.BlockSpec((tm, d), lambda i: (i, 0)),
        out_shape=jax.ShapeDtypeStruct((t, d), F32),
        scratch_shapes=[pltpu.VMEM((tm, dff), BF16)],
        compiler_params=_params("parallel"),
        name="ffn",
    )(xt, g_pre.reshape(1, d), g_post.reshape(1, d), w_perm, w_out.astype(BF16))


def _inproj_body(x_ref, g_ref, w_ref, *o_refs, splits):
    xn = _rms(x_ref[...], g_ref[...]).astype(BF16)
    for o_ref, (start, width) in zip(o_refs, splits):
        o_ref[...] = _dot(xn, w_ref[:, start:start + width]).astype(o_ref.dtype)


def _hyb_inproj(xt, g, w_in, d_ssm, conv_dim, n_ssm_heads, d_fox, n_fox_heads):
    t, d = xt.shape
    tm = min(TOKEN_TILE, t)
    o_z, o_xbc = 0, d_ssm
    o_dt = o_xbc + conv_dim
    o_q = o_dt + n_ssm_heads
    o_f = o_q + 3 * d_fox
    pad = V7X_LANES - n_ssm_heads - n_fox_heads
    w_perm = jnp.concatenate(
        [w_in[:, o_z:o_dt], w_in[:, o_q:o_f], w_in[:, o_dt:o_q], w_in[:, o_f:],
         jnp.zeros((d, pad), w_in.dtype)], axis=1).astype(BF16)
    widths = (d_ssm, conv_dim, d_fox, d_fox, d_fox, V7X_LANES)
    dtypes = (F32, F32, BF16, BF16, BF16, F32)
    splits, start = [], 0
    for w in widths:
        splits.append((start, w))
        start += w
    return pl.pallas_call(
        functools.partial(_inproj_body, splits=tuple(splits)),
        grid=(t // tm,),
        in_specs=[pl.BlockSpec((tm, d), lambda i: (i, 0)), _const_spec((1, d)),
                  _const_spec((d, start))],
        out_specs=[pl.BlockSpec((tm, w), lambda i: (i, 0)) for w in widths],
        out_shape=[jax.ShapeDtypeStruct((t, w), dt) for w, dt in zip(widths, dtypes)],
        compiler_params=_params("parallel"),
        name="hyb_inproj",
    )(xt, g.reshape(1, d), w_perm)


def _ssd_body(xbc_ref, z_ref, dtf_ref, convw_ref, convb_ref, bias_ref, arow_ref, dskip_ref,
              ng_ref, tri_ref, expand_ref,
              y_ref, cum_ref, cumt_ref,
              buf_ref, state_ref, carry_ref, carryt_ref, *, n_heads, n_fox_heads):
    L = SSD_CHUNK
    hd = SSM_HEAD_DIM
    ns = SSM_STATE
    d_ssm = n_heads * hd
    gw = d_ssm // SSM_GROUPS
    heads_per_group = n_heads // SSM_GROUPS

    @pl.when(pl.program_id(1) == 0)
    def _():
        buf_ref[0:V7X_SUBLANES, :] = jnp.zeros((V7X_SUBLANES, buf_ref.shape[1]), F32)
        state_ref[...] = jnp.zeros(state_ref.shape, F32)
        carry_ref[...] = jnp.zeros(carry_ref.shape, F32)
        carryt_ref[...] = jnp.zeros(carryt_ref.shape, F32)

    xcur = xbc_ref[0]
    buf_ref[V7X_SUBLANES:V7X_SUBLANES + L, :] = xcur
    cw = convw_ref[...]
    conv = convb_ref[...] + cw[CONV_K - 1:CONV_K] * xcur
    for k in range(CONV_K - 1):
        conv = conv + cw[k:k + 1] * buf_ref[pl.ds(V7X_SUBLANES - (CONV_K - 1) + k, L), :]
    buf_ref[0:V7X_SUBLANES, :] = buf_ref[L:L + V7X_SUBLANES, :]
    act = _silu(conv)
    xs = act[:, :d_ssm]
    bm = act[:, d_ssm:d_ssm + SSM_GROUPS * ns]
    cm = act[:, d_ssm + SSM_GROUPS * ns:]

    lane = lax.broadcasted_iota(jnp.int32, (L, V7X_LANES), 1)
    is_dt = lane < n_heads
    v = dtf_ref[0] + bias_ref[...]
    sp = _softplus(jnp.where(is_dt, v, -v))
    val = jnp.where(is_dt, sp, -sp)
    da = val * arow_ref[...]
    cs = _dot_exact_lhs(tri_ref[...], da)
    cst = cs.T

    exp_a = jnp.exp(cs)
    decay = jnp.exp(cs[L - 1:L, :] - cs)
    expand = expand_ref[...]
    dt_x = _dot_exact_rhs(val, expand)
    exp_a_x = _dot_exact_rhs(exp_a, expand)
    decay_x = _dot_exact_rhs(decay, expand)
    xdt = xs * dt_x

    row_i = lax.broadcasted_iota(jnp.int32, (L, L), 0)
    col_i = lax.broadcasted_iota(jnp.int32, (L, L), 1)
    causal = row_i >= col_i
    first_half = lane < hd

    y_parts = []
    for g in range(SSM_GROUPS):
        bg = bm[:, g * ns:(g + 1) * ns]
        cg = cm[:, g * ns:(g + 1) * ns].astype(BF16)
        cb = lax.dot_general(cg, bg.astype(BF16), (((1,), (1,)), ((), ())),
                             preferred_element_type=F32)
        gs = slice(g * gw, (g + 1) * gw)
        s_prev = state_ref[g]
        y_off = _dot(cg, s_prev.astype(BF16)) * exp_a_x[:, gs]
        contrib = _dot(bg.T.astype(BF16), (xdt[:, gs] * decay_x[:, gs]).astype(BF16))
        state_ref[g] = s_prev * exp_a_x[L - 1:L, gs] + contrib
        for pr in range(heads_per_group // 2):
            c0 = g * gw + pr * 2 * hd
            x_pair = xdt[:, c0:c0 + 2 * hd]
            acc = None
            for j in range(2):
                h = g * heads_per_group + 2 * pr + j
                seg = cs[:, h:h + 1] - cst[h:h + 1, :]
                lmat = jnp.exp(jnp.where(causal, seg, -jnp.inf))
                m = (lmat * cb).astype(BF16)
                keep = first_half if j == 0 else jnp.logical_not(first_half)
                part = _dot(m, jnp.where(keep, x_pair, 0.0).astype(BF16))
                acc = part if acc is None else acc + part
            y_parts.append(acc + y_off[:, c0 - g * gw:c0 - g * gw + 2 * hd])
    y = jnp.concatenate(y_parts, axis=1) + dskip_ref[...] * xs
    y = y * _silu(z_ref[0])
    normed = []
    for g in range(SSM_GROUPS):
        yg = y[:, g * gw:(g + 1) * gw]
        normed.append(yg * lax.rsqrt(jnp.mean(yg * yg, axis=-1, keepdims=True) + NORM_EPS))
    y_ref[0] = (jnp.concatenate(normed, axis=1) * ng_ref[...]).astype(y_ref.dtype)

    cum = cs + carry_ref[...]
    cum_ref[0] = cum
    carry_ref[...] = cum[L - 1:L, :]
    cumt = cst + carryt_ref[...]
    cumt_ref[0] = cumt[n_heads:n_heads + n_fox_heads, :]
    carryt_ref[...] = jnp.broadcast_to(cumt[:, L - 1:L], carryt_ref.shape)


def _ssd(xbc, z, dtf, conv_w, conv_b, dt_bias, a_log, d_skip, norm_g, fox_b_f):
    b, s, conv_dim = xbc.shape
    d_ssm = z.shape[-1]
    n_heads = dt_bias.shape[0]
    n_fox = fox_b_f.shape[0]
    L = SSD_CHUNK
    pad = V7X_LANES - n_heads - n_fox
    bias_row = jnp.concatenate([dt_bias, fox_b_f, jnp.zeros((pad,), F32)]).reshape(1, V7X_LANES)
    a_row = jnp.concatenate([-jnp.exp(a_log), jnp.ones((n_fox,), F32),
                             jnp.zeros((pad,), F32)]).reshape(1, V7X_LANES)
    d_row = jnp.repeat(d_skip, SSM_HEAD_DIM).reshape(1, d_ssm)
    tri = (jnp.arange(L)[:, None] >= jnp.arange(L)[None, :]).astype(BF16)
    expand = (jnp.arange(V7X_LANES)[:, None] == (jnp.arange(d_ssm)[None, :] // SSM_HEAD_DIM)).astype(BF16)
    blk = lambda w: pl.BlockSpec((1, L, w), lambda i, c: (i, c, 0))
    return pl.pallas_call(
        functools.partial(_ssd_body, n_heads=n_heads, n_fox_heads=n_fox),
        grid=(b, s // L),
        in_specs=[blk(conv_dim), blk(d_ssm), blk(V7X_LANES),
                  _const_spec((CONV_K, conv_dim)), _const_spec((1, conv_dim)),
                  _const_spec((1, V7X_LANES)), _const_spec((1, V7X_LANES)),
                  _const_spec((1, d_ssm)), _const_spec((1, d_ssm)),
                  _const_spec((L, L)), _const_spec((V7X_LANES, d_ssm))],
        out_specs=[blk(d_ssm), blk(V7X_LANES),
                   pl.BlockSpec((1, n_fox, L), lambda i, c: (i, 0, c))],
        out_shape=[jax.ShapeDtypeStruct((b, s, d_ssm), BF16),
                   jax.ShapeDtypeStruct((b, s, V7X_LANES), F32),
                   jax.ShapeDtypeStruct((b, n_fox, s), F32)],
        scratch_shapes=[pltpu.VMEM((L + V7X_SUBLANES, conv_dim), F32),
                        pltpu.VMEM((SSM_GROUPS, SSM_STATE, d_ssm // SSM_GROUPS), F32),
                        pltpu.VMEM((1, V7X_LANES), F32),
                        pltpu.VMEM((V7X_LANES, L), F32)],
        compiler_params=_params("parallel", "arbitrary"),
        name="ssd",
    )(xbc, z, dtf, conv_w, conv_b.reshape(1, conv_dim), bias_row, a_row, d_row,
      norm_g.reshape(1, d_ssm), tri, expand)


def _fox_body(q_ref, k_ref, v_ref, cum_ref, cumt_ref, o_ref, *, n_heads, blk, cum_col0, scale):
    i = pl.program_id(1)
    hd = FOX_HEAD_DIM
    row_i = lax.broadcasted_iota(jnp.int32, (blk, blk), 0)
    col_i = lax.broadcasted_iota(jnp.int32, (blk, blk), 1)
    causal = row_i >= col_i
    for h in range(n_heads):
        hs = slice(h * hd, (h + 1) * hd)
        q = q_ref[0, :, hs]
        cq = cum_ref[0, :, cum_col0 + h:cum_col0 + h + 1]

        def step(j, carry, masked):
            m, l, acc = carry
            start = pl.multiple_of(j * blk, blk)
            k = k_ref[0, pl.ds(start, blk), hs]
            v = v_ref[0, pl.ds(start, blk), hs]
            ck = cumt_ref[0, h:h + 1, pl.ds(start, blk)]
            s = lax.dot_general(q, k, (((1,), (1,)), ((), ())),
                                preferred_element_type=F32) * scale + (cq - ck)
            if masked:
                s = jnp.where(causal, s, -jnp.inf)
            m_new = jnp.maximum(m, jnp.max(s, axis=1, keepdims=True))
            alpha = jnp.exp(m - m_new)
            p = jnp.exp(s - m_new)
            l = alpha * l + jnp.sum(p, axis=1, keepdims=True)
            acc = alpha * acc + _dot(p.astype(BF16), v)
            return m_new, l, acc

        init = (jnp.full((blk, 1), -jnp.inf, F32), jnp.zeros((blk, 1), F32),
                jnp.zeros((blk, hd), F32))
        carry = lax.fori_loop(0, i, functools.partial(step, masked=False), init)
        _, l, acc = step(i, carry, True)
        o_ref[0, :, hs] = (acc / l).astype(o_ref.dtype)


def _fox(q, k, v, cum, cumt, n_ssm_heads):
    b, s, d_fox = q.shape
    n_heads = cumt.shape[1]
    blk = min(FOX_BLOCK, s)
    qspec = pl.BlockSpec((1, blk, d_fox), lambda bi, i: (bi, i, 0))
    kvspec = pl.BlockSpec((1, s, d_fox), lambda bi, i: (bi, 0, 0))
    return pl.pallas_call(
        functools.partial(_fox_body, n_heads=n_heads, blk=blk, cum_col0=n_ssm_heads,
                          scale=FOX_HEAD_DIM ** -0.5),
        grid=(b, s // blk),
        in_specs=[qspec, kvspec, kvspec,
                  pl.BlockSpec((1, blk, V7X_LANES), lambda bi, i: (bi, i, 0)),
                  pl.BlockSpec((1, n_heads, s), lambda bi, i: (bi, 0, 0))],
        out_specs=qspec,
        out_shape=jax.ShapeDtypeStruct((b, s, d_fox), BF16),
        compiler_params=_params("parallel", "arbitrary"),
        name="fox",
    )(q, k, v, cum, cumt)


def _outproj_body(y_ref, o_ref, x_ref, w_ref, g_ref, out_ref):
    d_y = y_ref.shape[1]
    h = _dot(y_ref[...], w_ref[:d_y, :]) + _dot(o_ref[...], w_ref[d_y:, :])
    out_ref[...] = x_ref[...] + _rms(h, g_ref[...])


def _hyb_outproj(y, o, xt, w_out, g):
    t, d = xt.shape
    tm = min(TOKEN_TILE, t)
    d_y, d_o = y.shape[1], o.shape[1]
    row = lambda w: pl.BlockSpec((tm, w), lambda i: (i, 0))
    return pl.pallas_call(
        _outproj_body,
        grid=(t // tm,),
        in_specs=[row(d_y), row(d_o), row(d), _const_spec((d_y + d_o, d)), _const_spec((1, d))],
        out_specs=row(d),
        out_shape=jax.ShapeDtypeStruct((t, d), F32),
        compiler_params=_params("parallel"),
        name="hyb_outproj",
    )(y, o, xt, w_out.astype(BF16), g.reshape(1, d))


def _hybrid_layer(xt, b, s, g_pre, g_post, w_in, conv_w, conv_b, dt_bias, a_log, d_skip,
                  ssm_norm_g, fox_b_f, w_out):
    d_ssm = ssm_norm_g.shape[0]
    conv_dim = conv_w.shape[1]
    n_ssm_heads = dt_bias.shape[0]
    n_fox_heads = fox_b_f.shape[0]
    d_fox = n_fox_heads * FOX_HEAD_DIM
    z, xbc, q, k, v, dtf = _hyb_inproj(xt, g_pre, w_in, d_ssm, conv_dim, n_ssm_heads,
                                       d_fox, n_fox_heads)
    r3 = lambda a: a.reshape(b, s, a.shape[-1])
    y, cum, cumt = _ssd(r3(xbc), r3(z), r3(dtf), conv_w, conv_b, dt_bias, a_log, d_skip,
                        ssm_norm_g, fox_b_f)
    o = _fox(r3(q), r3(k), r3(v), cum, cumt, n_ssm_heads)
    return _hyb_outproj(y.reshape(b * s, d_ssm), o.reshape(b * s, d_fox), xt, w_out, g_post)


def _rec_body(x_ref, gpre_ref, win_ref, convw_ref, convb_ref, wax_ref, ba_ref, bx_ref, lam_ref,
              wout_ref, gpost_ref, o_ref, buf_ref, a_ref, b_ref, h_ref, *, tt):
    d_rnn = lam_ref.shape[1]
    n_blocks = d_rnn // RNN_BLOCK

    @pl.when(pl.program_id(1) == 0)
    def _():
        buf_ref[0:V7X_SUBLANES, :] = jnp.zeros((V7X_SUBLANES, d_rnn), F32)
        h_ref[...] = jnp.zeros(h_ref.shape, F32)

    x = x_ref[0]
    xn = _rms(x, gpre_ref[...]).astype(BF16)
    gate = jax.nn.gelu(_dot(xn, win_ref[:, :d_rnn]), approximate=True)
    xr = _dot(xn, win_ref[:, d_rnn:])

    buf_ref[V7X_SUBLANES:V7X_SUBLANES + tt, :] = xr
    cw = convw_ref[...]
    xc = convb_ref[...] + cw[CONV_K - 1:CONV_K] * xr
    for k in range(CONV_K - 1):
        xc = xc + cw[k:k + 1] * buf_ref[pl.ds(V7X_SUBLANES - (CONV_K - 1) + k, tt), :]
    buf_ref[0:V7X_SUBLANES, :] = buf_ref[tt:tt + V7X_SUBLANES, :]

    xcb = xc.astype(BF16)
    r_parts, i_parts = [], []
    for n in range(n_blocks):
        ri = _dot(xcb[:, n * RNN_BLOCK:(n + 1) * RNN_BLOCK], wax_ref[n])
        r_parts.append(ri[:, :RNN_BLOCK])
        i_parts.append(ri[:, RNN_BLOCK:])
    r = jax.nn.sigmoid(jnp.concatenate(r_parts, axis=1) + ba_ref[...])
    ig = jax.nn.sigmoid(jnp.concatenate(i_parts, axis=1) + bx_ref[...])
    log_a = RG_LRU_C * r * (-_softplus(-lam_ref[...]))
    a = jnp.exp(log_a)
    u = jnp.sqrt(jnp.tanh(-log_a) * (1.0 + a * a)) * (ig * xc)

    rmod = lax.broadcasted_iota(jnp.int32, (tt, d_rnn), 0) & (V7X_SUBLANES - 1)
    for sh in (1, 2, 4):
        m = rmod >= sh
        a_sh = jnp.where(m, pltpu.roll(a, sh, 0), 1.0)
        u_sh = jnp.where(m, pltpu.roll(u, sh, 0), 0.0)
        u = a * u_sh + u
        a = a * a_sh
    a_ref[...] = a
    b_ref[...] = u

    def group(gi, h):
        off = pl.multiple_of(gi * V7X_SUBLANES, V7X_SUBLANES)
        hr = a_ref[pl.ds(off, V7X_SUBLANES), :] * h + b_ref[pl.ds(off, V7X_SUBLANES), :]
        b_ref[pl.ds(off, V7X_SUBLANES), :] = hr
        return hr[V7X_SUBLANES - 1:V7X_SUBLANES, :]

    h_ref[...] = lax.fori_loop(0, tt // V7X_SUBLANES, group, h_ref[...])
    y = (b_ref[...] * gate).astype(BF16)
    o_ref[0] = x + _rms(_dot(y, wout_ref[...]), gpost_ref[...])


def _rec_layer(xt, b, s, g_pre, g_post, w_in, conv_w, conv_b, w_a, b_a, w_x, b_x, lam, w_out):
    d = xt.shape[1]
    d_rnn = lam.shape[0]
    tt = min(REC_TILE, s)
    n_blocks = w_a.shape[0]
    w_ax = jnp.concatenate([w_a, w_x], axis=2).astype(BF16)
    row1 = lambda a: a.reshape(1, -1)
    blk = pl.BlockSpec((1, tt, d), lambda bi, ti: (bi, ti, 0))
    out = pl.pallas_call(
        functools.partial(_rec_body, tt=tt),
        grid=(b, s // tt),
        in_specs=[blk, _const_spec((1, d)), _const_spec((d, 2 * d_rnn)),
                  _const_spec((CONV_K, d_rnn)), _const_spec((1, d_rnn)),
                  _const_spec((n_blocks, RNN_BLOCK, 2 * RNN_BLOCK)),
                  _const_spec((1, d_rnn)), _const_spec((1, d_rnn)), _const_spec((1, d_rnn)),
                  _const_spec((d_rnn, d)), _const_spec((1, d))],
        out_specs=blk,
        out_shape=jax.ShapeDtypeStruct((b, s, d), F32),
        scratch_shapes=[pltpu.VMEM((tt + V7X_SUBLANES, d_rnn), F32),
                        pltpu.VMEM((tt, d_rnn), F32), pltpu.VMEM((tt, d_rnn), F32),
                        pltpu.VMEM((1, d_rnn), F32)],
        compiler_params=_params("parallel", "arbitrary"),
        name="rec",
    )(xt.reshape(b, s, d), row1(g_pre), w_in.astype(BF16), conv_w, row1(conv_b), w_ax,
      row1(b_a), row1(b_x), row1(lam), w_out.astype(BF16), row1(g_post))
    return out.reshape(b * s, d)


def kernel(x, norm_g, ffn_w_in, ffn_w_out, hyb_w_in, ssm_conv_w, ssm_conv_b, ssm_dt_bias, ssm_a_log, ssm_d, ssm_norm_g, fox_b_f, hyb_w_out, rec_w_in, rec_conv_w, rec_conv_b, rec_w_a, rec_b_a, rec_w_x, rec_b_x, rec_lambda, rec_w_out):
    b, s, d = x.shape
    depth = norm_g.shape[0]
    xt = x.reshape(b * s, d)
    for layer in range(depth):
        g = norm_g[layer]
        xt = _ffn(xt, g[0], g[1], ffn_w_in[layer, 0], ffn_w_out[layer, 0])
        if layer % 2 == 0:
            i = layer // 2
            xt = _hybrid_layer(xt, b, s, g[2], g[3], hyb_w_in[i], ssm_conv_w[i], ssm_conv_b[i],
                               ssm_dt_bias[i], ssm_a_log[i], ssm_d[i], ssm_norm_g[i],
                               fox_b_f[i], hyb_w_out[i])
        else:
            j = layer // 2
            xt = _rec_layer(xt, b, s, g[2], g[3], rec_w_in[j], rec_conv_w[j], rec_conv_b[j],
                            rec_w_a[j], rec_b_a[j], rec_w_x[j], rec_b_x[j], rec_lambda[j],
                            rec_w_out[j])
        xt = _ffn(xt, g[4], g[5], ffn_w_in[layer, 1], ffn_w_out[layer, 1])
    return xt.reshape(b, s, d)
```

```python
import functools

import jax
import jax.numpy as jnp
from jax import lax
from jax.experimental import pallas as pl
from jax.experimental.pallas import tpu as pltpu

F32 = jnp.float32
BF16 = jnp.bfloat16

NORM_EPS = 1e-6
CONV_K = 4
SSM_HEAD_DIM = 64
SSM_GROUPS = 2
SSM_STATE = 128
SSD_CHUNK = 128
FOX_HEAD_DIM = 128
RG_LRU_C = 8.0
RNN_BLOCK = 128
LOG2E = 1.4426950408889634

V7X_LANES = 128
V7X_SUBLANES = 8
VMEM_LIMIT_BYTES = 56 * 1024 * 1024

TOKEN_TILE = 512
FFN_CHUNK = 256
FOX_BLOCK = 256
REC_TILE = 512


def _params(*semantics, flags=None):
    return pltpu.CompilerParams(dimension_semantics=semantics,
                                vmem_limit_bytes=VMEM_LIMIT_BYTES, flags=flags)


def _const_spec(shape):
    zeros = (0,) * len(shape)
    return pl.BlockSpec(shape, lambda *_: zeros)


def _rms(x, g):
    return x * lax.rsqrt(jnp.mean(x * x, axis=-1, keepdims=True) + NORM_EPS) * g


def _silu(x):
    return x * jax.nn.sigmoid(x)


def _softplus(x):
    return jnp.maximum(x, 0.0) + jnp.log1p(jnp.exp(-jnp.abs(x)))


def _dot(a, b):
    return jnp.dot(a, b, preferred_element_type=F32)


def _split3(x):
    hi = x.astype(BF16)
    r = x - hi.astype(F32)
    mid = r.astype(BF16)
    lo = (r - mid.astype(F32)).astype(BF16)
    return hi, mid, lo


def _dot_exact_lhs(m, x):
    hi, mid, lo = _split3(x)
    return _dot(m, hi) + _dot(m, mid) + _dot(m, lo)


def _dot_exact_rhs(x, m):
    hi, mid, lo = _split3(x)
    return _dot(hi, m) + _dot(mid, m) + _dot(lo, m)


def _ffn_body(x_ref, gpre_ref, gpost_ref, win_ref, wout_ref, o_ref, h_ref, *, fc, n_chunks):
    x = x_ref[...]
    xn = _rms(x, gpre_ref[...]).astype(BF16)
    dff = fc * n_chunks
    for c in range(n_chunks):
        gate = _dot(xn, win_ref[:, fc * c:fc * (c + 1)])
        up = _dot(xn, win_ref[:, dff + fc * c:dff + fc * (c + 1)])
        h_ref[:, fc * c:fc * (c + 1)] = (_silu(gate) * up).astype(BF16)
    y = _dot(h_ref[...], wout_ref[...])
    o_ref[...] = x + 0.5 * _rms(y, gpost_ref[...])


def _ffn(xt, g_pre, g_post, w_in, w_out):
    t, d = xt.shape
    dff = w_out.shape[0]
    fc = FFN_CHUNK
    n_chunks = dff // fc
    assert n_chunks * fc == dff
    tm = min(TOKEN_TILE, t)
    return pl.pallas_call(
        functools.partial(_ffn_body, fc=fc, n_chunks=n_chunks),
        grid=(t // tm,),
        in_specs=[pl.BlockSpec((tm, d), lambda i: (i, 0)),
                  _const_spec((1, d)), _const_spec((1, d)),
                  _const_spec((d, 2 * dff)), _const_spec((dff, d))],
        out_specs=pl.BlockSpec((tm, d), lambda i: (i, 0)),
        out_shape=jax.ShapeDtypeStruct((t, d), F32),
        scratch_shapes=[pltpu.VMEM((tm, dff), BF16)],
        compiler_params=_params("parallel"),
        name="ffn",
    )(xt, g_pre.reshape(1, d), g_post.reshape(1, d), w_in, w_out)


def _inproj_body(x_ref, g_ref, w_ref, wvt_ref, *o_refs, splits):
    xn = _rms(x_ref[...], g_ref[...]).astype(BF16)
    for o_ref, (start, width) in zip(o_refs[:-1], splits):
        o_ref[...] = _dot(xn, w_ref[:, start:start + width]).astype(o_ref.dtype)
    vt = lax.dot_general(wvt_ref[...], xn, (((1,), (1,)), ((), ())), preferred_element_type=F32)
    o_refs[-1][...] = vt.astype(o_refs[-1].dtype)


def _hyb_inproj(xt, g, w_in, d_ssm, conv_dim, n_ssm_heads, d_fox, n_fox_heads):
    t, d = xt.shape
    tm = min(TOKEN_TILE, t)
    o_xbc = d_ssm
    o_dt = o_xbc + conv_dim
    o_q = o_dt + n_ssm_heads
    o_v = o_q + 2 * d_fox
    o_f = o_v + d_fox
    pad = V7X_LANES - n_ssm_heads - n_fox_heads
    w_perm = jnp.concatenate(
        [w_in[:, :o_dt], w_in[:, o_q:o_v], w_in[:, o_dt:o_q], w_in[:, o_f:],
         jnp.zeros((d, pad), w_in.dtype)], axis=1).astype(BF16)
    w_vt = w_in[:, o_v:o_f].T.astype(BF16)
    widths = (d_ssm, conv_dim, d_fox, d_fox, V7X_LANES)
    dtypes = (F32, F32, BF16, BF16, F32)
    splits, start = [], 0
    for w in widths:
        splits.append((start, w))
        start += w
    return pl.pallas_call(
        functools.partial(_inproj_body, splits=tuple(splits)),
        grid=(t // tm,),
        in_specs=[pl.BlockSpec((tm, d), lambda i: (i, 0)), _const_spec((1, d)),
                  _const_spec((d, start)), _const_spec((d_fox, d))],
        out_specs=[pl.BlockSpec((tm, w), lambda i: (i, 0)) for w in widths]
        + [pl.BlockSpec((d_fox, tm), lambda i: (0, i))],
        out_shape=[jax.ShapeDtypeStruct((t, w), dt) for w, dt in zip(widths, dtypes)]
        + [jax.ShapeDtypeStruct((d_fox, t), BF16)],
        compiler_params=_params("parallel"),
        name="hyb_inproj",
    )(xt, g.reshape(1, d), w_perm, w_vt)


def _ssd_body(xbc_ref, z_ref, dtf_ref, convw_ref, convb_ref, bias_ref, arow_ref, dskip_ref,
              ng_ref, tri_ref, expand_ref,
              y_ref, cum_ref, cumt_ref,
              buf_ref, state_ref, carry_ref, carryt_ref, *, n_heads, n_fox_heads):
    L = SSD_CHUNK
    hd = SSM_HEAD_DIM
    ns = SSM_STATE
    d_ssm = n_heads * hd
    gw = d_ssm // SSM_GROUPS
    heads_per_group = n_heads // SSM_GROUPS

    @pl.when(pl.program_id(1) == 0)
    def _():
        buf_ref[0:V7X_SUBLANES, :] = jnp.zeros((V7X_SUBLANES, buf_ref.shape[1]), F32)
        state_ref[...] = jnp.zeros(state_ref.shape, F32)
        carry_ref[...] = jnp.zeros(carry_ref.shape, F32)
        carryt_ref[...] = jnp.zeros(carryt_ref.shape, F32)

    xcur = xbc_ref[0]
    buf_ref[V7X_SUBLANES:V7X_SUBLANES + L, :] = xcur
    cw = convw_ref[...]
    conv = convb_ref[...] + cw[CONV_K - 1:CONV_K] * xcur
    for k in range(CONV_K - 1):
        conv = conv + cw[k:k + 1] * buf_ref[pl.ds(V7X_SUBLANES - (CONV_K - 1) + k, L), :]
    buf_ref[0:V7X_SUBLANES, :] = buf_ref[L:L + V7X_SUBLANES, :]
    act = _silu(conv)
    xs = act[:, :d_ssm]
    bm = act[:, d_ssm:d_ssm + SSM_GROUPS * ns]
    cm = act[:, d_ssm + SSM_GROUPS * ns:]

    lane = lax.broadcasted_iota(jnp.int32, (L, V7X_LANES), 1)
    is_dt = lane < n_heads
    v = dtf_ref[0] + bias_ref[...]
    sp = _softplus(jnp.where(is_dt, v, -v))
    val = jnp.where(is_dt, sp, -sp)
    da = val * arow_ref[...]
    cs = _dot_exact_lhs(tri_ref[...], da)
    cst = cs.T

    exp_a = jnp.exp(cs)
    decay = jnp.exp(cs[L - 1:L, :] - cs)
    expand = expand_ref[...]
    dt_x = _dot_exact_rhs(val, expand)
    exp_a_x = _dot_exact_rhs(exp_a, expand)
    decay_x = _dot_exact_rhs(decay, expand)
    xdt = xs * dt_x

    row_i = lax.broadcasted_iota(jnp.int32, (L, L), 0)
    col_i = lax.broadcasted_iota(jnp.int32, (L, L), 1)
    causal = row_i >= col_i
    first_half = lane < hd

    y_parts = []
    for g in range(SSM_GROUPS):
        bg = bm[:, g * ns:(g + 1) * ns]
        cg = cm[:, g * ns:(g + 1) * ns].astype(BF16)
        cb = lax.dot_general(cg, bg.astype(BF16), (((1,), (1,)), ((), ())),
                             preferred_element_type=F32)
        gs = slice(g * gw, (g + 1) * gw)
        s_prev = state_ref[g]
        y_off = _dot(cg, s_prev.astype(BF16)) * exp_a_x[:, gs]
        contrib = _dot(bg.T.astype(BF16), (xdt[:, gs] * decay_x[:, gs]).astype(BF16))
        state_ref[g] = s_prev * exp_a_x[L - 1:L, gs] + contrib
        for pr in range(heads_per_group // 2):
            c0 = g * gw + pr * 2 * hd
            x_pair = xdt[:, c0:c0 + 2 * hd]
            acc = None
            for j in range(2):
                h = g * heads_per_group + 2 * pr + j
                seg = cs[:, h:h + 1] - cst[h:h + 1, :]
                lmat = jnp.exp(jnp.where(causal, seg, -jnp.inf))
                m = (lmat * cb).astype(BF16)
                keep = first_half if j == 0 else jnp.logical_not(first_half)
                part = _dot(m, jnp.where(keep, x_pair, 0.0).astype(BF16))
                acc = part if acc is None else acc + part
            y_parts.append(acc + y_off[:, c0 - g * gw:c0 - g * gw + 2 * hd])
    y = jnp.concatenate(y_parts, axis=1) + dskip_ref[...] * xs
    y = y * _silu(z_ref[0])
    normed = []
    for g in range(SSM_GROUPS):
        yg = y[:, g * gw:(g + 1) * gw]
        normed.append(yg * lax.rsqrt(jnp.mean(yg * yg, axis=-1, keepdims=True) + NORM_EPS))
    y_ref[0] = (jnp.concatenate(normed, axis=1) * ng_ref[...]).astype(y_ref.dtype)

    cum = cs + carry_ref[...]
    cum_ref[0] = cum * LOG2E
    carry_ref[...] = cum[L - 1:L, :]
    cumt = cst + carryt_ref[...]
    cumt_ref[0] = cumt[n_heads:n_heads + n_fox_heads, :] * LOG2E
    carryt_ref[...] = jnp.broadcast_to(cumt[:, L - 1:L], carryt_ref.shape)


def _ssd(xbc, z, dtf, conv_w, conv_b, dt_bias, a_log, d_skip, norm_g, fox_b_f):
    b, s, conv_dim = xbc.shape
    d_ssm = z.shape[-1]
    n_heads = dt_bias.shape[0]
    n_fox = fox_b_f.shape[0]
    L = SSD_CHUNK
    pad = V7X_LANES - n_heads - n_fox
    bias_row = jnp.concatenate([dt_bias, fox_b_f, jnp.zeros((pad,), F32)]).reshape(1, V7X_LANES)
    a_row = jnp.concatenate([-jnp.exp(a_log), jnp.ones((n_fox,), F32),
                             jnp.zeros((pad,), F32)]).reshape(1, V7X_LANES)
    d_row = jnp.repeat(d_skip, SSM_HEAD_DIM).reshape(1, d_ssm)
    tri = (jnp.arange(L)[:, None] >= jnp.arange(L)[None, :]).astype(BF16)
    expand = (jnp.arange(V7X_LANES)[:, None] == (jnp.arange(d_ssm)[None, :] // SSM_HEAD_DIM)).astype(BF16)
    blk = lambda w: pl.BlockSpec((1, L, w), lambda i, c: (i, c, 0))
    return pl.pallas_call(
        functools.partial(_ssd_body, n_heads=n_heads, n_fox_heads=n_fox),
        grid=(b, s // L),
        in_specs=[blk(conv_dim), blk(d_ssm), blk(V7X_LANES),
                  _const_spec((CONV_K, conv_dim)), _const_spec((1, conv_dim)),
                  _const_spec((1, V7X_LANES)), _const_spec((1, V7X_LANES)),
                  _const_spec((1, d_ssm)), _const_spec((1, d_ssm)),
                  _const_spec((L, L)), _const_spec((V7X_LANES, d_ssm))],
        out_specs=[blk(d_ssm), blk(V7X_LANES),
                   pl.BlockSpec((1, n_fox, L), lambda i, c: (i, 0, c))],
        out_shape=[jax.ShapeDtypeStruct((b, s, d_ssm), BF16),
                   jax.ShapeDtypeStruct((b, s, V7X_LANES), F32),
                   jax.ShapeDtypeStruct((b, n_fox, s), F32)],
        scratch_shapes=[pltpu.VMEM((L + V7X_SUBLANES, conv_dim), F32),
                        pltpu.VMEM((SSM_GROUPS, SSM_STATE, d_ssm // SSM_GROUPS), F32),
                        pltpu.VMEM((1, V7X_LANES), F32),
                        pltpu.VMEM((V7X_LANES, L), F32)],
        compiler_params=_params("parallel", "arbitrary"),
        name="ssd",
    )(xbc, z, dtf, conv_w, conv_b.reshape(1, conv_dim), bias_row, a_row, d_row,
      norm_g.reshape(1, d_ssm), tri, expand)


def _fox_body(q_ref, k_ref, vt_ref, cum_ref, cumt_ref, o_ref, s_ref, m_ref, l_ref, acc_ref, *,
              n_heads, blk, cum_col0, scale):
    i = pl.program_id(1)
    hd = FOX_HEAD_DIM
    heads = range(n_heads)
    key_i = lax.broadcasted_iota(jnp.int32, (blk, blk), 0)
    qry_i = lax.broadcasted_iota(jnp.int32, (blk, blk), 1)
    causal = key_i <= qry_i
    m_ref[...] = jnp.full(m_ref.shape, -jnp.inf, F32)
    l_ref[...] = jnp.zeros(l_ref.shape, F32)
    acc_ref[...] = jnp.zeros(acc_ref.shape, F32)
    cq2 = cumt_ref[0]

    def scores(slot, j, masked):
        start = pl.multiple_of(j * blk, blk)
        for h in heads:
            hs = slice(h * hd, (h + 1) * hd)
            k = k_ref[0, pl.ds(start, blk), hs]
            ck2 = cum_ref[0, pl.ds(start, blk), cum_col0 + h:cum_col0 + h + 1]
            s = lax.dot_general(k, q_ref[0, :, hs], (((1,), (1,)), ((), ())),
                                preferred_element_type=F32) * (scale * LOG2E) - ck2
            if masked:
                s = jnp.where(causal, s, -jnp.inf)
            s_ref[slot, h] = s

    def softmax_pv(slot, j):
        start = pl.multiple_of(j * blk, blk)
        for h in heads:
            s = s_ref[slot, h]
            cq2h = cq2[h:h + 1]
            m_old = m_ref[h, 0:1, :]
            m_new = jnp.maximum(m_old, cq2h + jnp.max(s, axis=0, keepdims=True))
            alpha = jnp.exp2(m_old - m_new)
            p = jnp.exp2(s - (m_new - cq2h))
            l_ref[h, 0:1, :] = alpha * l_ref[h, 0:1, :] + jnp.sum(p, axis=0, keepdims=True)
            m_ref[h, 0:1, :] = m_new
            vt = vt_ref[h * hd:(h + 1) * hd, pl.ds(start, blk)]
            acc_ref[h] = alpha * acc_ref[h] + _dot(vt, p.astype(BF16))

    scores(0, i, True)

    def pair(u, carry):
        scores(1, 2 * u, False)
        softmax_pv(0, jnp.where(u == 0, i, 2 * u - 1))
        scores(0, 2 * u + 1, False)
        softmax_pv(1, 2 * u)
        return carry

    n_pairs = i // 2
    lax.fori_loop(0, n_pairs, pair, 0)
    last0 = jnp.where(i < 2, i, 2 * n_pairs - 1)

    @pl.when(i % 2 == 1)
    def _():
        scores(1, i - 1, False)
        softmax_pv(0, last0)
        softmax_pv(1, i - 1)

    @pl.when(i % 2 == 0)
    def _():
        softmax_pv(0, last0)

    for h in heads:
        o = acc_ref[h] / l_ref[h, 0:1, :]
        o_ref[0, :, h * hd:(h + 1) * hd] = o.T.astype(o_ref.dtype)


def _fox(q, k, vt, cum, cumt, n_ssm_heads):
    b, s, d_fox = q.shape
    n_heads = cumt.shape[1]
    blk = min(FOX_BLOCK, s)
    qspec = pl.BlockSpec((1, blk, d_fox), lambda bi, i: (bi, i, 0))
    return pl.pallas_call(
        functools.partial(_fox_body, n_heads=n_heads, blk=blk, cum_col0=n_ssm_heads,
                          scale=FOX_HEAD_DIM ** -0.5),
        grid=(b, s // blk),
        in_specs=[qspec,
                  pl.BlockSpec((1, s, d_fox), lambda bi, i: (bi, 0, 0)),
                  pl.BlockSpec((d_fox, s), lambda bi, i: (0, bi)),
                  pl.BlockSpec((1, s, V7X_LANES), lambda bi, i: (bi, 0, 0)),
                  pl.BlockSpec((1, n_heads, blk), lambda bi, i: (bi, 0, i))],
        out_specs=qspec,
        out_shape=jax.ShapeDtypeStruct((b, s, d_fox), BF16),
        scratch_shapes=[pltpu.VMEM((2, n_heads, blk, blk), F32),
                        pltpu.VMEM((n_heads, V7X_SUBLANES, blk), F32),
                        pltpu.VMEM((n_heads, V7X_SUBLANES, blk), F32),
                        pltpu.VMEM((n_heads, FOX_HEAD_DIM, blk), F32)],
        compiler_params=_params("parallel", "arbitrary"),
        name="fox",
    )(q, k, vt, cum, cumt)


def _outproj_body(y_ref, o_ref, x_ref, w_ref, g_ref, out_ref):
    d_y = y_ref.shape[1]
    h = _dot(y_ref[...], w_ref[:d_y, :]) + _dot(o_ref[...], w_ref[d_y:, :])
    out_ref[...] = x_ref[...] + _rms(h, g_ref[...])


def _hyb_outproj(y, o, xt, w_out, g):
    t, d = xt.shape
    tm = min(TOKEN_TILE, t)
    d_y, d_o = y.shape[1], o.shape[1]
    row = lambda w: pl.BlockSpec((tm, w), lambda i: (i, 0))
    return pl.pallas_call(
        _outproj_body,
        grid=(t // tm,),
        in_specs=[row(d_y), row(d_o), row(d), _const_spec((d_y + d_o, d)), _const_spec((1, d))],
        out_specs=row(d),
        out_shape=jax.ShapeDtypeStruct((t, d), F32),
        compiler_params=_params("parallel"),
        name="hyb_outproj",
    )(y, o, xt, w_out.astype(BF16), g.reshape(1, d))


def _hybrid_layer(xt, b, s, g_pre, g_post, w_in, conv_w, conv_b, dt_bias, a_log, d_skip,
                  ssm_norm_g, fox_b_f, w_out):
    d_ssm = ssm_norm_g.shape[0]
    conv_dim = conv_w.shape[1]
    n_ssm_heads = dt_bias.shape[0]
    n_fox_heads = fox_b_f.shape[0]
    d_fox = n_fox_heads * FOX_HEAD_DIM
    z, xbc, q, k, dtf, vt = _hyb_inproj(xt, g_pre, w_in, d_ssm, conv_dim, n_ssm_heads,
                                        d_fox, n_fox_heads)
    r3 = lambda a: a.reshape(b, s, a.shape[-1])
    y, cum, cumt = _ssd(r3(xbc), r3(z), r3(dtf), conv_w, conv_b, dt_bias, a_log, d_skip,
                        ssm_norm_g, fox_b_f)
    o = _fox(r3(q), r3(k), vt, cum, cumt, n_ssm_heads)
    return _hyb_outproj(y.reshape(b * s, d_ssm), o.reshape(b * s, d_fox), xt, w_out, g_post)


def _rec_body(x_ref, gpre_ref, win_ref, convw_ref, convb_ref, wax_ref, ba_ref, bx_ref, lam_ref,
              wout_ref, gpost_ref, o_ref, buf_ref, a_ref, b_ref, h_ref, *, tt):
    d_rnn = lam_ref.shape[1]
    n_blocks = d_rnn // RNN_BLOCK

    @pl.when(pl.program_id(1) == 0)
    def _():
        buf_ref[0:V7X_SUBLANES, :] = jnp.zeros((V7X_SUBLANES, d_rnn), F32)
        h_ref[...] = jnp.zeros(h_ref.shape, F32)

    x = x_ref[0]
    xn = _rms(x, gpre_ref[...]).astype(BF16)
    gate = jax.nn.gelu(_dot(xn, win_ref[:, :d_rnn]), approximate=True)
    xr = _dot(xn, win_ref[:, d_rnn:])

    buf_ref[V7X_SUBLANES:V7X_SUBLANES + tt, :] = xr
    cw = convw_ref[...]
    xc = convb_ref[...] + cw[CONV_K - 1:CONV_K] * xr
    for k in range(CONV_K - 1):
        xc = xc + cw[k:k + 1] * buf_ref[pl.ds(V7X_SUBLANES - (CONV_K - 1) + k, tt), :]
    buf_ref[0:V7X_SUBLANES, :] = buf_ref[tt:tt + V7X_SUBLANES, :]

    xcb = xc.astype(BF16)
    r_parts, i_parts = [], []
    for n in range(n_blocks):
        ri = _dot(xcb[:, n * RNN_BLOCK:(n + 1) * RNN_BLOCK], wax_ref[n])
        r_parts.append(ri[:, :RNN_BLOCK])
        i_parts.append(ri[:, RNN_BLOCK:])
    r = jax.nn.sigmoid(jnp.concatenate(r_parts, axis=1) + ba_ref[...])
    ig = jax.nn.sigmoid(jnp.concatenate(i_parts, axis=1) + bx_ref[...])
    log_a = RG_LRU_C * r * (-_softplus(-lam_ref[...]))
    a = jnp.exp(log_a)
    u = jnp.sqrt(jnp.tanh(-log_a) * (1.0 + a * a)) * (ig * xc)

    groups = (tt // V7X_SUBLANES, V7X_SUBLANES, d_rnn)
    a = a.reshape(groups)
    u = u.reshape(groups)
    rmod = lax.broadcasted_iota(jnp.int32, groups, 1)
    for sh in (1, 2, 4):
        m = rmod >= sh
        a_sh = jnp.where(m, pltpu.roll(a, sh, 1), 1.0)
        u_sh = jnp.where(m, pltpu.roll(u, sh, 1), 0.0)
        u = a * u_sh + u
        a = a * a_sh
    a_ref[...] = a.reshape(tt, d_rnn)
    b_ref[...] = u.reshape(tt, d_rnn)

    def group(gi, h):
        off = pl.multiple_of(gi * V7X_SUBLANES, V7X_SUBLANES)
        hr = a_ref[pl.ds(off, V7X_SUBLANES), :] * h + b_ref[pl.ds(off, V7X_SUBLANES), :]
        b_ref[pl.ds(off, V7X_SUBLANES), :] = hr
        return hr[V7X_SUBLANES - 1:V7X_SUBLANES, :]

    h_ref[...] = lax.fori_loop(0, tt // V7X_SUBLANES, group, h_ref[...])
    y = (b_ref[...] * gate).astype(BF16)
    o_ref[0] = x + _rms(_dot(y, wout_ref[...]), gpost_ref[...])


def _rec_layer(xt, b, s, g_pre, g_post, w_in, conv_w, conv_b, w_a, b_a, w_x, b_x, lam, w_out):
    d = xt.shape[1]
    d_rnn = lam.shape[0]
    tt = min(REC_TILE, s)
    n_blocks = w_a.shape[0]
    w_ax = jnp.concatenate([w_a, w_x], axis=2).astype(BF16)
    row1 = lambda a: a.reshape(1, -1)
    blk = pl.BlockSpec((1, tt, d), lambda bi, ti: (bi, ti, 0))
    out = pl.pallas_call(
        functools.partial(_rec_body, tt=tt),
        grid=(b, s // tt),
        in_specs=[blk, _const_spec((1, d)), _const_spec((d, 2 * d_rnn)),
                  _const_spec((CONV_K, d_rnn)), _const_spec((1, d_rnn)),
                  _const_spec((n_blocks, RNN_BLOCK, 2 * RNN_BLOCK)),
                  _const_spec((1, d_rnn)), _const_spec((1, d_rnn)), _const_spec((1, d_rnn)),
                  _const_spec((d_rnn, d)), _const_spec((1, d))],
        out_specs=blk,
        out_shape=jax.ShapeDtypeStruct((b, s, d), F32),
        scratch_shapes=[pltpu.VMEM((tt + V7X_SUBLANES, d_rnn), F32),
                        pltpu.VMEM((tt, d_rnn), F32), pltpu.VMEM((tt, d_rnn), F32),
                        pltpu.VMEM((1, d_rnn), F32)],
        compiler_params=_params("parallel", "arbitrary"),
        name="rec",
    )(xt.reshape(b, s, d), row1(g_pre), w_in.astype(BF16), conv_w, row1(conv_b), w_ax,
      row1(b_a), row1(b_x), row1(lam), w_out.astype(BF16), row1(g_post))
    return out.reshape(b * s, d)


def kernel(x, norm_g, ffn_w_in, ffn_w_out, hyb_w_in, ssm_conv_w, ssm_conv_b, ssm_dt_bias, ssm_a_log, ssm_d, ssm_norm_g, fox_b_f, hyb_w_out, rec_w_in, rec_conv_w, rec_conv_b, rec_w_a, rec_b_a, rec_w_x, rec_b_x, rec_lambda, rec_w_out):
    b, s, d = x.shape
    depth = norm_g.shape[0]
    xt = x.reshape(b * s, d)
    ffn_w_in = ffn_w_in.astype(BF16)
    ffn_w_out = ffn_w_out.astype(BF16)
    for layer in range(depth):
        g = norm_g[layer]
        xt = _ffn(xt, g[0], g[1], ffn_w_in[layer, 0], ffn_w_out[layer, 0])
        if layer % 2 == 0:
            i = layer // 2
            xt = _hybrid_layer(xt, b, s, g[2], g[3], hyb_w_in[i], ssm_conv_w[i], ssm_conv_b[i],
                               ssm_dt_bias[i], ssm_a_log[i], ssm_d[i], ssm_norm_g[i],
                               fox_b_f[i], hyb_w_out[i])
        else:
            j = layer // 2
            xt = _rec_layer(xt, b, s, g[2], g[3], rec_w_in[j], rec_conv_w[j], rec_conv_b[j],
                            rec_w_a[j], rec_b_a[j], rec_w_x[j], rec_b_x[j], rec_lambda[j],
                            rec_w_out[j])
        xt = _ffn(xt, g[4], g[5], ffn_w_in[layer, 1], ffn_w_out[layer, 1])
    return xt.reshape(b, s, d)
```

```python
import functools

import jax
import jax.numpy as jnp
from jax import lax
from jax.experimental import pallas as pl
from jax.experimental.pallas import tpu as pltpu

F32 = jnp.float32
BF16 = jnp.bfloat16

NORM_EPS = 1e-6
CONV_K = 4
SSM_HEAD_DIM = 64
SSM_GROUPS = 2
SSM_STATE = 128
SSD_CHUNK = 128
FOX_HEAD_DIM = 128
RG_LRU_C = 8.0
RNN_BLOCK = 128
LOG2E = 1.4426950408889634

V7X_LANES = 128
V7X_SUBLANES = 8
V7X_BF16_SUBLANES = 16
VMEM_LIMIT_BYTES = 56 * 1024 * 1024

TOKEN_TILE = 512
FFN_CHUNK = 256
FOX_BLOCK = 256
REC_TILE = 512


def _params(*semantics, flags=None):
    return pltpu.CompilerParams(dimension_semantics=semantics,
                                vmem_limit_bytes=VMEM_LIMIT_BYTES, flags=flags)


def _const_spec(shape):
    zeros = (0,) * len(shape)
    return pl.BlockSpec(shape, lambda *_: zeros)


def _rms(x, g):
    return x * lax.rsqrt(jnp.mean(x * x, axis=-1, keepdims=True) + NORM_EPS) * g


def _silu(x):
    return x * jax.nn.sigmoid(x)


def _softplus(x):
    return jnp.maximum(x, 0.0) + jnp.log1p(jnp.exp(-jnp.abs(x)))


def _dot(a, b):
    return jnp.dot(a, b, preferred_element_type=F32)


def _split(x, terms):
    parts = []
    for _ in range(terms - 1):
        part = x.astype(BF16)
        parts.append(part)
        x = x - part.astype(F32)
    parts.append(x.astype(BF16))
    return parts


def _dot_exact_lhs(m, x, terms=3):
    return sum(_dot(m, part) for part in _split(x, terms))


def _dot_exact_rhs(x, m, terms=3):
    return sum(_dot(part, m) for part in _split(x, terms))


def _ffn_body(x_ref, gpre_ref, gpost_ref, win_ref, wout_ref, o_ref, h_ref, *, fc, n_chunks):
    x = x_ref[...]
    xn = _rms(x, gpre_ref[...]).astype(BF16)
    dff = fc * n_chunks
    for c in range(n_chunks):
        gate = _dot(xn, win_ref[:, fc * c:fc * (c + 1)])
        up = _dot(xn, win_ref[:, dff + fc * c:dff + fc * (c + 1)])
        h_ref[:, fc * c:fc * (c + 1)] = (_silu(gate) * up).astype(BF16)
    y = _dot(h_ref[...], wout_ref[...])
    o_ref[...] = x + 0.5 * _rms(y, gpost_ref[...])


def _ffn(xt, g_pre, g_post, w_in, w_out):
    t, d = xt.shape
    dff = w_out.shape[0]
    fc = FFN_CHUNK
    n_chunks = dff // fc
    assert n_chunks * fc == dff
    tm = min(TOKEN_TILE, t)
    return pl.pallas_call(
        functools.partial(_ffn_body, fc=fc, n_chunks=n_chunks),
        grid=(t // tm,),
        in_specs=[pl.BlockSpec((tm, d), lambda i: (i, 0)),
                  _const_spec((1, d)), _const_spec((1, d)),
                  _const_spec((d, 2 * dff)), _const_spec((dff, d))],
        out_specs=pl.BlockSpec((tm, d), lambda i: (i, 0)),
        out_shape=jax.ShapeDtypeStruct((t, d), F32),
        scratch_shapes=[pltpu.VMEM((tm, dff), BF16)],
        compiler_params=_params("parallel"),
        name="ffn",
    )(xt, g_pre.reshape(1, d), g_post.reshape(1, d), w_in, w_out)


def _inproj_body(x_ref, g_ref, w_ref, wvt_ref, *o_refs, splits):
    xn = _rms(x_ref[...], g_ref[...]).astype(BF16)
    for o_ref, (start, width, mult) in zip(o_refs[:-1], splits):
        y = _dot(xn, w_ref[:, start:start + width])
        o_ref[...] = (y if mult is None else y * mult).astype(o_ref.dtype)
    vt = lax.dot_general(wvt_ref[...], xn, (((1,), (1,)), ((), ())), preferred_element_type=F32)
    o_refs[-1][...] = vt.astype(o_refs[-1].dtype)


def _hyb_inproj(xt, g, w_in, d_ssm, conv_dim, n_ssm_heads, d_fox, n_fox_heads):
    t, d = xt.shape
    tm = min(TOKEN_TILE, t)
    o_xbc = d_ssm
    o_dt = o_xbc + conv_dim
    o_q = o_dt + n_ssm_heads
    o_v = o_q + 2 * d_fox
    o_f = o_v + d_fox
    pad = V7X_LANES - n_ssm_heads - n_fox_heads
    w_perm = jnp.concatenate(
        [w_in[:, :o_dt], w_in[:, o_q:o_v], w_in[:, o_dt:o_q], w_in[:, o_f:],
         jnp.zeros((d, pad), w_in.dtype)], axis=1).astype(BF16)
    w_vt = w_in[:, o_v:o_f].T.astype(BF16)
    widths = (d_ssm, conv_dim, d_fox, d_fox, V7X_LANES)
    dtypes = (F32, F32, BF16, BF16, F32)
    mults = (None, None, FOX_HEAD_DIM ** -0.5 * LOG2E, None, None)
    splits, start = [], 0
    for w, mult in zip(widths, mults):
        splits.append((start, w, mult))
        start += w
    return pl.pallas_call(
        functools.partial(_inproj_body, splits=tuple(splits)),
        grid=(t // tm,),
        in_specs=[pl.BlockSpec((tm, d), lambda i: (i, 0)), _const_spec((1, d)),
                  _const_spec((d, start)), _const_spec((d_fox, d))],
        out_specs=[pl.BlockSpec((tm, w), lambda i: (i, 0)) for w in widths]
        + [pl.BlockSpec((d_fox, tm), lambda i: (0, i))],
        out_shape=[jax.ShapeDtypeStruct((t, w), dt) for w, dt in zip(widths, dtypes)]
        + [jax.ShapeDtypeStruct((d_fox, t), BF16)],
        compiler_params=_params("parallel"),
        name="hyb_inproj",
    )(xt, g.reshape(1, d), w_perm, w_vt)


def _ssd_body(xbc_ref, z_ref, dtf_ref, convw_ref, convb_ref, bias_ref, arow_ref, dskip_ref,
              ng_ref, tri_ref, expand_ref,
              y_ref, cum_ref, cumt_ref,
              buf_ref, state_ref, carry_ref, carryt_ref, *, n_heads, n_fox_heads):
    L = SSD_CHUNK
    hd = SSM_HEAD_DIM
    ns = SSM_STATE
    d_ssm = n_heads * hd
    gw = d_ssm // SSM_GROUPS
    heads_per_group = n_heads // SSM_GROUPS

    @pl.when(pl.program_id(1) == 0)
    def _():
        buf_ref[0:V7X_SUBLANES, :] = jnp.zeros((V7X_SUBLANES, buf_ref.shape[1]), F32)
        state_ref[...] = jnp.zeros(state_ref.shape, F32)
        carry_ref[...] = jnp.zeros(carry_ref.shape, F32)
        carryt_ref[...] = jnp.zeros(carryt_ref.shape, F32)

    xcur = xbc_ref[0]
    buf_ref[V7X_SUBLANES:V7X_SUBLANES + L, :] = xcur
    cw = convw_ref[...]
    conv = convb_ref[...] + cw[CONV_K - 1:CONV_K] * xcur
    for k in range(CONV_K - 1):
        conv = conv + cw[k:k + 1] * buf_ref[pl.ds(V7X_SUBLANES - (CONV_K - 1) + k, L), :]
    buf_ref[0:V7X_SUBLANES, :] = buf_ref[L:L + V7X_SUBLANES, :]
    act = _silu(conv)
    xs = act[:, :d_ssm]
    bm = act[:, d_ssm:d_ssm + SSM_GROUPS * ns]
    cm = act[:, d_ssm + SSM_GROUPS * ns:]

    lane = lax.broadcasted_iota(jnp.int32, (L, V7X_LANES), 1)
    is_dt = lane < n_heads
    v = dtf_ref[0] + bias_ref[...]
    sp = _softplus(jnp.where(is_dt, v, -v))
    val = jnp.where(is_dt, sp, -sp)
    da = val * arow_ref[...]
    cs = _dot_exact_lhs(tri_ref[...], da)
    cst = cs.T

    expand = expand_ref[...]
    xdt = xs * _dot_exact_rhs(val, expand, terms=2)
    cs_x = _dot_exact_rhs(cs, expand)
    exp_a_x = jnp.exp(cs_x)
    decay_x = jnp.exp(cs_x[L - 1:L, :] - cs_x)

    row_i = lax.broadcasted_iota(jnp.int32, (L, L), 0)
    col_i = lax.broadcasted_iota(jnp.int32, (L, L), 1)
    causal = row_i >= col_i
    first_half = lane < hd

    y_parts = []
    for g in range(SSM_GROUPS):
        bg = bm[:, g * ns:(g + 1) * ns]
        cg = cm[:, g * ns:(g + 1) * ns].astype(BF16)
        cb = lax.dot_general(cg, bg.astype(BF16), (((1,), (1,)), ((), ())),
                             preferred_element_type=F32)
        gs = slice(g * gw, (g + 1) * gw)
        s_prev = state_ref[g]
        y_off = _dot(cg, s_prev.astype(BF16)) * exp_a_x[:, gs]
        contrib = _dot(bg.T.astype(BF16), (xdt[:, gs] * decay_x[:, gs]).astype(BF16))
        state_ref[g] = s_prev * exp_a_x[L - 1:L, gs] + contrib
        for pr in range(heads_per_group // 2):
            c0 = g * gw + pr * 2 * hd
            x_pair = xdt[:, c0:c0 + 2 * hd]
            acc = None
            for j in range(2):
                h = g * heads_per_group + 2 * pr + j
                seg = cs[:, h:h + 1] - cst[h:h + 1, :]
                lmat = jnp.exp(jnp.where(causal, seg, -jnp.inf))
                m = (lmat * cb).astype(BF16)
                keep = first_half if j == 0 else jnp.logical_not(first_half)
                part = _dot(m, jnp.where(keep, x_pair, 0.0).astype(BF16))
                acc = part if acc is None else acc + part
            y_parts.append(acc + y_off[:, c0 - g * gw:c0 - g * gw + 2 * hd])
    y = jnp.concatenate(y_parts, axis=1) + dskip_ref[...] * xs
    y = y * _silu(z_ref[0])
    normed = []
    for g in range(SSM_GROUPS):
        yg = y[:, g * gw:(g + 1) * gw]
        normed.append(yg * lax.rsqrt(jnp.mean(yg * yg, axis=-1, keepdims=True) + NORM_EPS))
    y_ref[0] = (jnp.concatenate(normed, axis=1) * ng_ref[...]).astype(y_ref.dtype)

    cum = cs + carry_ref[...]
    cum_ref[0] = cum * LOG2E
    carry_ref[...] = cum[L - 1:L, :]
    cumt = cst + carryt_ref[...]
    cumt_ref[0] = cumt[n_heads:n_heads + n_fox_heads, :] * LOG2E
    carryt_ref[...] = jnp.broadcast_to(cumt[:, L - 1:L], carryt_ref.shape)


def _ssd(xbc, z, dtf, conv_w, conv_b, dt_bias, a_log, d_skip, norm_g, fox_b_f):
    b, s, conv_dim = xbc.shape
    d_ssm = z.shape[-1]
    n_heads = dt_bias.shape[0]
    n_fox = fox_b_f.shape[0]
    L = SSD_CHUNK
    pad = V7X_LANES - n_heads - n_fox
    bias_row = jnp.concatenate([dt_bias, fox_b_f, jnp.zeros((pad,), F32)]).reshape(1, V7X_LANES)
    a_row = jnp.concatenate([-jnp.exp(a_log), jnp.ones((n_fox,), F32),
                             jnp.zeros((pad,), F32)]).reshape(1, V7X_LANES)
    d_row = jnp.repeat(d_skip, SSM_HEAD_DIM).reshape(1, d_ssm)
    tri = (jnp.arange(L)[:, None] >= jnp.arange(L)[None, :]).astype(BF16)
    expand = (jnp.arange(V7X_LANES)[:, None] == (jnp.arange(d_ssm)[None, :] // SSM_HEAD_DIM)).astype(BF16)
    blk = lambda w: pl.BlockSpec((1, L, w), lambda i, c: (i, c, 0))
    return pl.pallas_call(
        functools.partial(_ssd_body, n_heads=n_heads, n_fox_heads=n_fox),
        grid=(b, s // L),
        in_specs=[blk(conv_dim), blk(d_ssm), blk(V7X_LANES),
                  _const_spec((CONV_K, conv_dim)), _const_spec((1, conv_dim)),
                  _const_spec((1, V7X_LANES)), _const_spec((1, V7X_LANES)),
                  _const_spec((1, d_ssm)), _const_spec((1, d_ssm)),
                  _const_spec((L, L)), _const_spec((V7X_LANES, d_ssm))],
        out_specs=[blk(d_ssm), blk(V7X_LANES),
                   pl.BlockSpec((1, n_fox, L), lambda i, c: (i, 0, c))],
        out_shape=[jax.ShapeDtypeStruct((b, s, d_ssm), BF16),
                   jax.ShapeDtypeStruct((b, s, V7X_LANES), F32),
                   jax.ShapeDtypeStruct((b, n_fox, s), F32)],
        scratch_shapes=[pltpu.VMEM((L + V7X_SUBLANES, conv_dim), F32),
                        pltpu.VMEM((SSM_GROUPS, SSM_STATE, d_ssm // SSM_GROUPS), F32),
                        pltpu.VMEM((1, V7X_LANES), F32),
                        pltpu.VMEM((V7X_LANES, L), F32)],
        compiler_params=_params("parallel", "arbitrary"),
        name="ssd",
    )(xbc, z, dtf, conv_w, conv_b.reshape(1, conv_dim), bias_row, a_row, d_row,
      norm_g.reshape(1, d_ssm), tri, expand)


def _fox_body(q_ref, k_ref, vt_ref, cum_ref, cumt_ref, o_ref, s_ref, m_ref, l_ref, acc_ref, *,
              n_heads, blk, cum_col0):
    i = pl.program_id(1)
    hd = FOX_HEAD_DIM
    heads = range(n_heads)
    key_i = lax.broadcasted_iota(jnp.int32, (blk, blk), 0)
    qry_i = lax.broadcasted_iota(jnp.int32, (blk, blk), 1)
    causal = key_i <= qry_i
    m_ref[...] = jnp.full(m_ref.shape, -jnp.inf, F32)
    l_ref[...] = jnp.zeros(l_ref.shape, F32)
    acc_ref[...] = jnp.zeros(acc_ref.shape, F32)
    cq2 = cumt_ref[0]
    ones_rows = jnp.ones((V7X_BF16_SUBLANES, blk), BF16)

    def scores(slot, j, masked):
        start = pl.multiple_of(j * blk, blk)
        for h in heads:
            hs = slice(h * hd, (h + 1) * hd)
            k = k_ref[0, pl.ds(start, blk), hs]
            ck2 = cum_ref[0, pl.ds(start, blk), cum_col0 + h:cum_col0 + h + 1]
            s = lax.dot_general(k, q_ref[0, :, hs], (((1,), (1,)), ((), ())),
                                preferred_element_type=F32) - ck2
            if masked:
                s = jnp.where(causal, s, -jnp.inf)
            s_ref[slot, h] = s

    def softmax_pv(slot, j):
        start = pl.multiple_of(j * blk, blk)
        for h in heads:
            s = s_ref[slot, h]
            cq2h = cq2[h:h + 1]
            m_old = m_ref[h, 0:1, :]
            m_new = jnp.maximum(m_old, cq2h + jnp.max(s, axis=0, keepdims=True))
            alpha = jnp.exp2(m_old - m_new)
            p = jnp.exp2(s - (m_new - cq2h)).astype(BF16)
            m_ref[h, 0:1, :] = m_new
            vt = jnp.concatenate([vt_ref[h * hd:(h + 1) * hd, pl.ds(start, blk)], ones_rows], axis=0)
            pv = _dot(vt, p)
            acc_ref[h] = alpha * acc_ref[h] + pv[:hd]
            l_ref[h, 0:1, :] = alpha * l_ref[h, 0:1, :] + pv[hd:hd + 1]

    scores(0, i, True)

    def pair(u, carry):
        scores(1, 2 * u, False)
        softmax_pv(0, jnp.where(u == 0, i, 2 * u - 1))
        scores(0, 2 * u + 1, False)
        softmax_pv(1, 2 * u)
        return carry

    n_pairs = i // 2
    lax.fori_loop(0, n_pairs, pair, 0)
    last0 = jnp.where(i < 2, i, 2 * n_pairs - 1)

    @pl.when(i % 2 == 1)
    def _():
        scores(1, i - 1, False)
        softmax_pv(0, last0)
        softmax_pv(1, i - 1)

    @pl.when(i % 2 == 0)
    def _():
        softmax_pv(0, last0)

    for h in heads:
        o = acc_ref[h] / l_ref[h, 0:1, :]
        o_ref[0, :, h * hd:(h + 1) * hd] = o.T.astype(o_ref.dtype)


def _fox(q, k, vt, cum, cumt, n_ssm_heads):
    b, s, d_fox = q.shape
    n_heads = cumt.shape[1]
    blk = min(FOX_BLOCK, s)
    qspec = pl.BlockSpec((1, blk, d_fox), lambda bi, i: (bi, i, 0))
    return pl.pallas_call(
        functools.partial(_fox_body, n_heads=n_heads, blk=blk, cum_col0=n_ssm_heads),
        grid=(b, s // blk),
        in_specs=[qspec,
                  pl.BlockSpec((1, s, d_fox), lambda bi, i: (bi, 0, 0)),
                  pl.BlockSpec((d_fox, s), lambda bi, i: (0, bi)),
                  pl.BlockSpec((1, s, V7X_LANES), lambda bi, i: (bi, 0, 0)),
                  pl.BlockSpec((1, n_heads, blk), lambda bi, i: (bi, 0, i))],
        out_specs=qspec,
        out_shape=jax.ShapeDtypeStruct((b, s, d_fox), BF16),
        scratch_shapes=[pltpu.VMEM((2, n_heads, blk, blk), F32),
                        pltpu.VMEM((n_heads, V7X_SUBLANES, blk), F32),
                        pltpu.VMEM((n_heads, V7X_SUBLANES, blk), F32),
                        pltpu.VMEM((n_heads, FOX_HEAD_DIM, blk), F32)],
        compiler_params=_params("parallel", "arbitrary"),
        name="fox",
    )(q, k, vt, cum, cumt)


def _outproj_body(y_ref, o_ref, x_ref, w_ref, g_ref, out_ref):
    d_y = y_ref.shape[1]
    h = _dot(y_ref[...], w_ref[:d_y, :]) + _dot(o_ref[...], w_ref[d_y:, :])
    out_ref[...] = x_ref[...] + _rms(h, g_ref[...])


def _hyb_outproj(y, o, xt, w_out, g):
    t, d = xt.shape
    tm = min(TOKEN_TILE, t)
    d_y, d_o = y.shape[1], o.shape[1]
    row = lambda w: pl.BlockSpec((tm, w), lambda i: (i, 0))
    return pl.pallas_call(
        _outproj_body,
        grid=(t // tm,),
        in_specs=[row(d_y), row(d_o), row(d), _const_spec((d_y + d_o, d)), _const_spec((1, d))],
        out_specs=row(d),
        out_shape=jax.ShapeDtypeStruct((t, d), F32),
        compiler_params=_params("parallel"),
        name="hyb_outproj",
    )(y, o, xt, w_out.astype(BF16), g.reshape(1, d))


def _hybrid_layer(xt, b, s, g_pre, g_post, w_in, conv_w, conv_b, dt_bias, a_log, d_skip,
                  ssm_norm_g, fox_b_f, w_out):
    d_ssm = ssm_norm_g.shape[0]
    conv_dim = conv_w.shape[1]
    n_ssm_heads = dt_bias.shape[0]
    n_fox_heads = fox_b_f.shape[0]
    d_fox = n_fox_heads * FOX_HEAD_DIM
    z, xbc, q, k, dtf, vt = _hyb_inproj(xt, g_pre, w_in, d_ssm, conv_dim, n_ssm_heads,
                                        d_fox, n_fox_heads)
    r3 = lambda a: a.reshape(b, s, a.shape[-1])
    y, cum, cumt = _ssd(r3(xbc), r3(z), r3(dtf), conv_w, conv_b, dt_bias, a_log, d_skip,
                        ssm_norm_g, fox_b_f)
    o = _fox(r3(q), r3(k), vt, cum, cumt, n_ssm_heads)
    return _hyb_outproj(y.reshape(b * s, d_ssm), o.reshape(b * s, d_fox), xt, w_out, g_post)


def _rec_body(x_ref, gpre_ref, win_ref, convw_ref, convb_ref, wax_ref, ba_ref, bx_ref, lam_ref,
              wout_ref, gpost_ref, o_ref, buf_ref, a_ref, b_ref, h_ref, *, tt):
    d_rnn = lam_ref.shape[1]
    n_blocks = d_rnn // RNN_BLOCK

    @pl.when(pl.program_id(1) == 0)
    def _():
        buf_ref[0:V7X_SUBLANES, :] = jnp.zeros((V7X_SUBLANES, d_rnn), F32)
        h_ref[...] = jnp.zeros(h_ref.shape, F32)

    x = x_ref[0]
    xn = _rms(x, gpre_ref[...]).astype(BF16)
    gate = jax.nn.gelu(_dot(xn, win_ref[:, :d_rnn]), approximate=True)
    xr = _dot(xn, win_ref[:, d_rnn:])

    buf_ref[V7X_SUBLANES:V7X_SUBLANES + tt, :] = xr
    cw = convw_ref[...]
    xc = convb_ref[...] + cw[CONV_K - 1:CONV_K] * xr
    for k in range(CONV_K - 1):
        xc = xc + cw[k:k + 1] * buf_ref[pl.ds(V7X_SUBLANES - (CONV_K - 1) + k, tt), :]
    buf_ref[0:V7X_SUBLANES, :] = buf_ref[tt:tt + V7X_SUBLANES, :]

    xcb = xc.astype(BF16)
    r_parts, i_parts = [], []
    for n in range(n_blocks):
        ri = _dot(xcb[:, n * RNN_BLOCK:(n + 1) * RNN_BLOCK], wax_ref[n])
        r_parts.append(ri[:, :RNN_BLOCK])
        i_parts.append(ri[:, RNN_BLOCK:])
    r = jax.nn.sigmoid(jnp.concatenate(r_parts, axis=1) + ba_ref[...])
    ig = jax.nn.sigmoid(jnp.concatenate(i_parts, axis=1) + bx_ref[...])
    log_a = RG_LRU_C * r * (-_softplus(-lam_ref[...]))
    a = jnp.exp(log_a)
    u = jnp.sqrt(jnp.tanh(-log_a) * (1.0 + a * a)) * (ig * xc)

    groups = (tt // V7X_SUBLANES, V7X_SUBLANES, d_rnn)
    a = a.reshape(groups)
    u = u.reshape(groups)
    rmod = lax.broadcasted_iota(jnp.int32, groups, 1)
    for sh in (1, 2, 4):
        m = rmod >= sh
        a_sh = jnp.where(m, pltpu.roll(a, sh, 1), 1.0)
        u_sh = jnp.where(m, pltpu.roll(u, sh, 1), 0.0)
        u = a * u_sh + u
        a = a * a_sh
    a_ref[...] = a.reshape(tt, d_rnn)
    b_ref[...] = u.reshape(tt, d_rnn)

    def group(gi, h):
        off = pl.multiple_of(gi * V7X_SUBLANES, V7X_SUBLANES)
        hr = a_ref[pl.ds(off, V7X_SUBLANES), :] * h + b_ref[pl.ds(off, V7X_SUBLANES), :]
        b_ref[pl.ds(off, V7X_SUBLANES), :] = hr
        return hr[V7X_SUBLANES - 1:V7X_SUBLANES, :]

    h_ref[...] = lax.fori_loop(0, tt // V7X_SUBLANES, group, h_ref[...])
    y = (b_ref[...] * gate).astype(BF16)
    o_ref[0] = x + _rms(_dot(y, wout_ref[...]), gpost_ref[...])


def _rec_layer(xt, b, s, g_pre, g_post, w_in, conv_w, conv_b, w_a, b_a, w_x, b_x, lam, w_out):
    d = xt.shape[1]
    d_rnn = lam.shape[0]
    tt = min(REC_TILE, s)
    n_blocks = w_a.shape[0]
    w_ax = jnp.concatenate([w_a, w_x], axis=2).astype(BF16)
    row1 = lambda a: a.reshape(1, -1)
    blk = pl.BlockSpec((1, tt, d), lambda bi, ti: (bi, ti, 0))
    out = pl.pallas_call(
        functools.partial(_rec_body, tt=tt),
        grid=(b, s // tt),
        in_specs=[blk, _const_spec((1, d)), _const_spec((d, 2 * d_rnn)),
                  _const_spec((CONV_K, d_rnn)), _const_spec((1, d_rnn)),
                  _const_spec((n_blocks, RNN_BLOCK, 2 * RNN_BLOCK)),
                  _const_spec((1, d_rnn)), _const_spec((1, d_rnn)), _const_spec((1, d_rnn)),
                  _const_spec((d_rnn, d)), _const_spec((1, d))],
        out_specs=blk,
        out_shape=jax.ShapeDtypeStruct((b, s, d), F32),
        scratch_shapes=[pltpu.VMEM((tt + V7X_SUBLANES, d_rnn), F32),
                        pltpu.VMEM((tt, d_rnn), F32), pltpu.VMEM((tt, d_rnn), F32),
                        pltpu.VMEM((1, d_rnn), F32)],
        compiler_params=_params("parallel", "arbitrary"),
        name="rec",
    )(xt.reshape(b, s, d), row1(g_pre), w_in.astype(BF16), conv_w, row1(conv_b), w_ax,
      row1(b_a), row1(b_x), row1(lam), w_out.astype(BF16), row1(g_post))
    return out.reshape(b * s, d)


def kernel(x, norm_g, ffn_w_in, ffn_w_out, hyb_w_in, ssm_conv_w, ssm_conv_b, ssm_dt_bias, ssm_a_log, ssm_d, ssm_norm_g, fox_b_f, hyb_w_out, rec_w_in, rec_conv_w, rec_conv_b, rec_w_a, rec_b_a, rec_w_x, rec_b_x, rec_lambda, rec_w_out):
    b, s, d = x.shape
    depth = norm_g.shape[0]
    xt = x.reshape(b * s, d)
    ffn_w_in = ffn_w_in.astype(BF16)
    ffn_w_out = ffn_w_out.astype(BF16)
    for layer in range(depth):
        g = norm_g[layer]
        xt = _ffn(xt, g[0], g[1], ffn_w_in[layer, 0], ffn_w_out[layer, 0])
        if layer % 2 == 0:
            i = layer // 2
            xt = _hybrid_layer(xt, b, s, g[2], g[3], hyb_w_in[i], ssm_conv_w[i], ssm_conv_b[i],
                               ssm_dt_bias[i], ssm_a_log[i], ssm_d[i], ssm_norm_g[i],
                               fox_b_f[i], hyb_w_out[i])
        else:
            j = layer // 2
            xt = _rec_layer(xt, b, s, g[2], g[3], rec_w_in[j], rec_conv_w[j], rec_conv_b[j],
                            rec_w_a[j], rec_b_a[j], rec_w_x[j], rec_b_x[j], rec_lambda[j],
                            rec_w_out[j])
        xt = _ffn(xt, g[4], g[5], ffn_w_in[layer, 1], ffn_w_out[layer, 1])
    return xt.reshape(b, s, d)
```

```python
import functools

import jax
import jax.numpy as jnp
from jax import lax
from jax.experimental import pallas as pl
from jax.experimental.pallas import tpu as pltpu

F32 = jnp.float32
BF16 = jnp.bfloat16

NORM_EPS = 1e-6
CONV_K = 4
SSM_HEAD_DIM = 64
SSM_GROUPS = 2
SSM_STATE = 128
SSD_CHUNK = 128
FOX_HEAD_DIM = 128
RG_LRU_C = 8.0
RNN_BLOCK = 128
LOG2E = 1.4426950408889634

V7X_LANES = 128
V7X_SUBLANES = 8
V7X_BF16_SUBLANES = 16
VMEM_LIMIT_BYTES = 56 * 1024 * 1024

TOKEN_TILE = 512
FFN_CHUNK = 256
FOX_QUERY_BLOCK = 512
REC_TILE = 512


def _params(*semantics):
    return pltpu.CompilerParams(dimension_semantics=semantics,
                                vmem_limit_bytes=VMEM_LIMIT_BYTES)


def _const_spec(shape):
    zeros = (0,) * len(shape)
    return pl.BlockSpec(shape, lambda *_: zeros)


def _rms(x, g):
    return x * lax.rsqrt(jnp.mean(x * x, axis=-1, keepdims=True) + NORM_EPS) * g


def _sigmoid(x):
    return 0.5 * jnp.tanh(0.5 * x) + 0.5


def _silu(x):
    return x * _sigmoid(x)


def _gelu_tanh(x):
    inner = x * (0.7978845608028654 + 0.035677408136300125 * (x * x))
    return x * (0.5 * jnp.tanh(inner) + 0.5)


def _softplus(x):
    return jnp.maximum(x, 0.0) + jnp.log1p(jnp.exp(-jnp.abs(x)))


def _dot(a, b):
    return jnp.dot(a, b, preferred_element_type=F32)


def _split(x, terms):
    parts = []
    for _ in range(terms - 1):
        part = x.astype(BF16)
        parts.append(part)
        x = x - part.astype(F32)
    parts.append(x.astype(BF16))
    return parts


def _dot_exact_lhs(m, x, terms=3):
    return sum(_dot(m, part) for part in _split(x, terms))


def _dot_exact_rhs(x, m, terms=3):
    return sum(_dot(part, m) for part in _split(x, terms))


def _causal_conv(buf_ref, x, w_ref, b_ref, rows):
    buf_ref[V7X_SUBLANES:V7X_SUBLANES + rows, :] = x
    w = w_ref[...]
    y = b_ref[...] + w[CONV_K - 1:CONV_K] * x
    for k in range(CONV_K - 1):
        y = y + w[k:k + 1] * buf_ref[pl.ds(V7X_SUBLANES - (CONV_K - 1) + k, rows), :]
    buf_ref[0:V7X_SUBLANES, :] = buf_ref[rows:rows + V7X_SUBLANES, :]
    return y


def _ffn_body(x_ref, gpre_ref, gpost_ref, win_ref, wout_ref, o_ref, h_ref, *, fc, n_chunks):
    x = x_ref[...]
    xn = _rms(x, gpre_ref[...]).astype(BF16)
    dff = fc * n_chunks
    for c in range(n_chunks):
        gate = _dot(xn, win_ref[:, fc * c:fc * (c + 1)])
        up = _dot(xn, win_ref[:, dff + fc * c:dff + fc * (c + 1)])
        h_ref[:, fc * c:fc * (c + 1)] = (_silu(gate) * up).astype(BF16)
    y = _dot(h_ref[...], wout_ref[...])
    o_ref[...] = x + 0.5 * _rms(y, gpost_ref[...])


def _ffn(xt, g_pre, g_post, w_in, w_out):
    t, d = xt.shape
    dff = w_out.shape[0]
    fc = FFN_CHUNK
    n_chunks = dff // fc
    assert n_chunks * fc == dff
    tm = min(TOKEN_TILE, t)
    return pl.pallas_call(
        functools.partial(_ffn_body, fc=fc, n_chunks=n_chunks),
        grid=(t // tm,),
        in_specs=[pl.BlockSpec((tm, d), lambda i: (i, 0)),
                  _const_spec((1, d)), _const_spec((1, d)),
                  _const_spec((d, 2 * dff)), _const_spec((dff, d))],
        out_specs=pl.BlockSpec((tm, d), lambda i: (i, 0)),
        out_shape=jax.ShapeDtypeStruct((t, d), F32),
        scratch_shapes=[pltpu.VMEM((tm, dff), BF16)],
        compiler_params=_params("parallel"),
        name="ffn",
    )(xt, g_pre.reshape(1, d), g_post.reshape(1, d), w_in, w_out)


def _inproj_body(x_ref, g_ref, w_ref, wvt_ref, *o_refs, splits):
    xn = _rms(x_ref[...], g_ref[...]).astype(BF16)
    for o_ref, (start, width, mult) in zip(o_refs[:-1], splits):
        y = _dot(xn, w_ref[:, start:start + width])
        o_ref[...] = (y if mult is None else y * mult).astype(o_ref.dtype)
    vt = lax.dot_general(wvt_ref[...], xn, (((1,), (1,)), ((), ())), preferred_element_type=F32)
    o_refs[-1][...] = vt.astype(o_refs[-1].dtype)


def _hyb_inproj(xt, g, w_in, d_ssm, conv_dim, n_ssm_heads, d_fox, n_fox_heads):
    t, d = xt.shape
    tm = min(TOKEN_TILE, t)
    o_xbc = d_ssm
    o_dt = o_xbc + conv_dim
    o_q = o_dt + n_ssm_heads
    o_v = o_q + 2 * d_fox
    o_f = o_v + d_fox
    pad = V7X_LANES - n_ssm_heads - n_fox_heads
    w_perm = jnp.concatenate(
        [w_in[:, :o_dt], w_in[:, o_q:o_v], w_in[:, o_dt:o_q], w_in[:, o_f:],
         jnp.zeros((d, pad), w_in.dtype)], axis=1).astype(BF16)
    w_vt = w_in[:, o_v:o_f].T.astype(BF16)
    widths = (d_ssm, conv_dim, d_fox, d_fox, V7X_LANES)
    dtypes = (F32, F32, BF16, BF16, F32)
    mults = (None, None, FOX_HEAD_DIM ** -0.5 * LOG2E, None, None)
    splits, start = [], 0
    for w, mult in zip(widths, mults):
        splits.append((start, w, mult))
        start += w
    return pl.pallas_call(
        functools.partial(_inproj_body, splits=tuple(splits)),
        grid=(t // tm,),
        in_specs=[pl.BlockSpec((tm, d), lambda i: (i, 0)), _const_spec((1, d)),
                  _const_spec((d, start)), _const_spec((d_fox, d))],
        out_specs=[pl.BlockSpec((tm, w), lambda i: (i, 0)) for w in widths]
        + [pl.BlockSpec((d_fox, tm), lambda i: (0, i))],
        out_shape=[jax.ShapeDtypeStruct((t, w), dt) for w, dt in zip(widths, dtypes)]
        + [jax.ShapeDtypeStruct((d_fox, t), BF16)],
        compiler_params=_params("parallel"),
        name="hyb_inproj",
    )(xt, g.reshape(1, d), w_perm, w_vt)


def _ssd_body(xbc_ref, z_ref, dtf_ref, convw_ref, convb_ref, bias_ref, arow_ref, dskip_ref,
              ng_ref, tri_ref, expand_ref,
              y_ref, cum_ref, cumt_ref,
              buf_ref, state_ref, carry_ref, carryt_ref, *, n_heads, n_fox_heads):
    L = SSD_CHUNK
    hd = SSM_HEAD_DIM
    ns = SSM_STATE
    d_ssm = n_heads * hd
    gw = d_ssm // SSM_GROUPS
    heads_per_group = n_heads // SSM_GROUPS

    @pl.when(pl.program_id(1) == 0)
    def _():
        buf_ref[0:V7X_SUBLANES, :] = jnp.zeros((V7X_SUBLANES, buf_ref.shape[1]), F32)
        state_ref[...] = jnp.zeros(state_ref.shape, F32)
        carry_ref[...] = jnp.zeros(carry_ref.shape, F32)
        carryt_ref[...] = jnp.zeros(carryt_ref.shape, F32)

    act = _silu(_causal_conv(buf_ref, xbc_ref[0], convw_ref, convb_ref, L))
    xs = act[:, :d_ssm]
    bm = act[:, d_ssm:d_ssm + SSM_GROUPS * ns]
    cm = act[:, d_ssm + SSM_GROUPS * ns:]

    lane = lax.broadcasted_iota(jnp.int32, (L, V7X_LANES), 1)
    is_dt = lane < n_heads
    v = dtf_ref[0] + bias_ref[...]
    sp = _softplus(jnp.where(is_dt, v, -v))
    val = jnp.where(is_dt, sp, -sp)
    da = val * arow_ref[...]
    cs = _dot_exact_lhs(tri_ref[...], da)
    cst = cs.T

    expand = expand_ref[...]
    xdt = xs * _dot_exact_rhs(val, expand, terms=2)
    cs_x = _dot_exact_rhs(cs, expand)
    exp_a_x = jnp.exp(cs_x)
    decay_x = jnp.exp(cs_x[L - 1:L, :] - cs_x)

    row_i = lax.broadcasted_iota(jnp.int32, (L, L), 0)
    col_i = lax.broadcasted_iota(jnp.int32, (L, L), 1)
    causal = row_i >= col_i
    first_half = lane < hd

    y_parts = []
    for g in range(SSM_GROUPS):
        bg = bm[:, g * ns:(g + 1) * ns]
        cg = cm[:, g * ns:(g + 1) * ns].astype(BF16)
        cb = lax.dot_general(cg, bg.astype(BF16), (((1,), (1,)), ((), ())),
                             preferred_element_type=F32)
        gs = slice(g * gw, (g + 1) * gw)
        s_prev = state_ref[g]
        y_off = _dot(cg, s_prev.astype(BF16)) * exp_a_x[:, gs]
        contrib = _dot(bg.T.astype(BF16), (xdt[:, gs] * decay_x[:, gs]).astype(BF16))
        state_ref[g] = s_prev * exp_a_x[L - 1:L, gs] + contrib
        for pr in range(heads_per_group // 2):
            c0 = g * gw + pr * 2 * hd
            x_pair = xdt[:, c0:c0 + 2 * hd]
            acc = None
            for j in range(2):
                h = g * heads_per_group + 2 * pr + j
                seg = cs[:, h:h + 1] - cst[h:h + 1, :]
                lmat = jnp.exp(jnp.where(causal, seg, -jnp.inf))
                m = (lmat * cb).astype(BF16)
                keep = first_half if j == 0 else jnp.logical_not(first_half)
                part = _dot(m, jnp.where(keep, x_pair, 0.0).astype(BF16))
                acc = part if acc is None else acc + part
            y_parts.append(acc + y_off[:, c0 - g * gw:c0 - g * gw + 2 * hd])
    y = jnp.concatenate(y_parts, axis=1) + dskip_ref[...] * xs
    y = y * _silu(z_ref[0])
    normed = []
    for g in range(SSM_GROUPS):
        yg = y[:, g * gw:(g + 1) * gw]
        normed.append(yg * lax.rsqrt(jnp.mean(yg * yg, axis=-1, keepdims=True) + NORM_EPS))
    y_ref[0] = (jnp.concatenate(normed, axis=1) * ng_ref[...]).astype(y_ref.dtype)

    cum = cs + carry_ref[...]
    cum_ref[0] = cum * LOG2E
    carry_ref[...] = cum[L - 1:L, :]
    cumt = cst + carryt_ref[...]
    cumt_ref[0] = cumt[n_heads:n_heads + n_fox_heads, :] * LOG2E
    carryt_ref[...] = jnp.broadcast_to(cumt[:, L - 1:L], carryt_ref.shape)


def _ssd(xbc, z, dtf, conv_w, conv_b, dt_bias, a_log, d_skip, norm_g, fox_b_f):
    b, s, conv_dim = xbc.shape
    d_ssm = z.shape[-1]
    n_heads = dt_bias.shape[0]
    n_fox = fox_b_f.shape[0]
    L = SSD_CHUNK
    pad = V7X_LANES - n_heads - n_fox
    bias_row = jnp.concatenate([dt_bias, fox_b_f, jnp.zeros((pad,), F32)]).reshape(1, V7X_LANES)
    a_row = jnp.concatenate([-jnp.exp(a_log), jnp.ones((n_fox,), F32),
                             jnp.zeros((pad,), F32)]).reshape(1, V7X_LANES)
    d_row = jnp.repeat(d_skip, SSM_HEAD_DIM).reshape(1, d_ssm)
    tri = (jnp.arange(L)[:, None] >= jnp.arange(L)[None, :]).astype(BF16)
    expand = (jnp.arange(V7X_LANES)[:, None] == (jnp.arange(d_ssm)[None, :] // SSM_HEAD_DIM)).astype(BF16)
    blk = lambda w: pl.BlockSpec((1, L, w), lambda i, c: (i, c, 0))
    return pl.pallas_call(
        functools.partial(_ssd_body, n_heads=n_heads, n_fox_heads=n_fox),
        grid=(b, s // L),
        in_specs=[blk(conv_dim), blk(d_ssm), blk(V7X_LANES),
                  _const_spec((CONV_K, conv_dim)), _const_spec((1, conv_dim)),
                  _const_spec((1, V7X_LANES)), _const_spec((1, V7X_LANES)),
                  _const_spec((1, d_ssm)), _const_spec((1, d_ssm)),
                  _const_spec((L, L)), _const_spec((V7X_LANES, d_ssm))],
        out_specs=[blk(d_ssm), blk(V7X_LANES),
                   pl.BlockSpec((1, n_fox, L), lambda i, c: (i, 0, c))],
        out_shape=[jax.ShapeDtypeStruct((b, s, d_ssm), BF16),
                   jax.ShapeDtypeStruct((b, s, V7X_LANES), F32),
                   jax.ShapeDtypeStruct((b, n_fox, s), F32)],
        scratch_shapes=[pltpu.VMEM((L + V7X_SUBLANES, conv_dim), F32),
                        pltpu.VMEM((SSM_GROUPS, SSM_STATE, d_ssm // SSM_GROUPS), F32),
                        pltpu.VMEM((1, V7X_LANES), F32),
                        pltpu.VMEM((V7X_LANES, L), F32)],
        compiler_params=_params("parallel", "arbitrary"),
        name="ssd",
    )(xbc, z, dtf, conv_w, conv_b.reshape(1, conv_dim), bias_row, a_row, d_row,
      norm_g.reshape(1, d_ssm), tri, expand)


def _fox_body(q_ref, k_ref, vt_ref, cum_ref, cumt_ref, o_ref, s_ref, bm_ref, m_ref, l_ref, acc_ref, *,
              n_heads, qb, kb, cum_col0):
    i = pl.program_id(1)
    hd = FOX_HEAD_DIM
    heads = range(n_heads)
    key_i = lax.broadcasted_iota(jnp.int32, (kb, qb), 0)
    qry_i = lax.broadcasted_iota(jnp.int32, (kb, qb), 1)
    m_ref[...] = jnp.full(m_ref.shape, -jnp.inf, F32)
    l_ref[...] = jnp.zeros(l_ref.shape, F32)
    acc_ref[...] = jnp.zeros(acc_ref.shape, F32)
    cq2 = cumt_ref[0]
    ones_rows = jnp.ones((V7X_BF16_SUBLANES, kb), BF16)

    def scores(slot, j, diag_offset=None):
        start = pl.multiple_of(j * kb, kb)
        for h in heads:
            hs = slice(h * hd, (h + 1) * hd)
            k = k_ref[0, pl.ds(start, kb), hs]
            ck2 = cum_ref[0, pl.ds(start, kb), cum_col0 + h:cum_col0 + h + 1]
            s = lax.dot_general(k, q_ref[0, :, hs], (((1,), (1,)), ((), ())),
                                preferred_element_type=F32) - ck2
            if diag_offset is not None:
                s = jnp.where(key_i + diag_offset <= qry_i, s, -jnp.inf)
            s_ref[slot, h] = s
            bm_ref[slot, h, 0:1, :] = jnp.max(s, axis=0, keepdims=True)

    def softmax_pv(slot, j):
        start = pl.multiple_of(j * kb, kb)
        for h in heads:
            cq2h = cq2[h:h + 1]
            m_old = m_ref[h, 0:1, :]
            m_new = jnp.maximum(m_old, cq2h + bm_ref[slot, h, 0:1, :])
            alpha = jnp.exp2(m_old - m_new)
            p = jnp.exp2(s_ref[slot, h] - (m_new - cq2h)).astype(BF16)
            m_ref[h, 0:1, :] = m_new
            vt = jnp.concatenate([vt_ref[h * hd:(h + 1) * hd, pl.ds(start, kb)], ones_rows], axis=0)
            pv = _dot(vt, p)
            acc_ref[h] = alpha * acc_ref[h] + pv[:hd]
            l_ref[h, 0:1, :] = alpha * l_ref[h, 0:1, :] + pv[hd:hd + 1]

    scores(0, 2 * i, diag_offset=0)
    scores(1, 2 * i + 1, diag_offset=kb)
    softmax_pv(0, 2 * i)

    def pair(u, carry):
        scores(0, 2 * u)
        softmax_pv(1, jnp.where(u == 0, 2 * i + 1, 2 * u - 1))
        scores(1, 2 * u + 1)
        softmax_pv(0, 2 * u)
        return carry

    lax.fori_loop(0, i, pair, 0)
    softmax_pv(1, jnp.where(i == 0, 1, 2 * i - 1))

    for h in heads:
        o = acc_ref[h] / l_ref[h, 0:1, :]
        o_ref[0, :, h * hd:(h + 1) * hd] = o.T.astype(o_ref.dtype)


def _fox(q, k, vt, cum, cumt, n_ssm_heads):
    b, s, d_fox = q.shape
    n_heads = cumt.shape[1]
    qb = min(FOX_QUERY_BLOCK, s)
    kb = qb // 2
    assert s % qb == 0
    qspec = pl.BlockSpec((1, qb, d_fox), lambda bi, i: (bi, i, 0))
    return pl.pallas_call(
        functools.partial(_fox_body, n_heads=n_heads, qb=qb, kb=kb, cum_col0=n_ssm_heads),
        grid=(b, s // qb),
        in_specs=[qspec,
                  pl.BlockSpec((1, s, d_fox), lambda bi, i: (bi, 0, 0)),
                  pl.BlockSpec((d_fox, s), lambda bi, i: (0, bi)),
                  pl.BlockSpec((1, s, V7X_LANES), lambda bi, i: (bi, 0, 0)),
                  pl.BlockSpec((1, n_heads, qb), lambda bi, i: (bi, 0, i))],
        out_specs=qspec,
        out_shape=jax.ShapeDtypeStruct((b, s, d_fox), BF16),
        scratch_shapes=[pltpu.VMEM((2, n_heads, kb, qb), F32),
                        pltpu.VMEM((2, n_heads, V7X_SUBLANES, qb), F32),
                        pltpu.VMEM((n_heads, V7X_SUBLANES, qb), F32),
                        pltpu.VMEM((n_heads, V7X_SUBLANES, qb), F32),
                        pltpu.VMEM((n_heads, FOX_HEAD_DIM, qb), F32)],
        compiler_params=_params("parallel", "arbitrary"),
        name="fox",
    )(q, k, vt, cum, cumt)


def _outproj_body(y_ref, o_ref, x_ref, w_ref, g_ref, out_ref):
    d_y = y_ref.shape[1]
    h = _dot(y_ref[...], w_ref[:d_y, :]) + _dot(o_ref[...], w_ref[d_y:, :])
    out_ref[...] = x_ref[...] + _rms(h, g_ref[...])


def _hyb_outproj(y, o, xt, w_out, g):
    t, d = xt.shape
    tm = min(TOKEN_TILE, t)
    d_y, d_o = y.shape[1], o.shape[1]
    row = lambda w: pl.BlockSpec((tm, w), lambda i: (i, 0))
    return pl.pallas_call(
        _outproj_body,
        grid=(t // tm,),
        in_specs=[row(d_y), row(d_o), row(d), _const_spec((d_y + d_o, d)), _const_spec((1, d))],
        out_specs=row(d),
        out_shape=jax.ShapeDtypeStruct((t, d), F32),
        compiler_params=_params("parallel"),
        name="hyb_outproj",
    )(y, o, xt, w_out.astype(BF16), g.reshape(1, d))


def _hybrid_layer(xt, b, s, g_pre, g_post, w_in, conv_w, conv_b, dt_bias, a_log, d_skip,
                  ssm_norm_g, fox_b_f, w_out):
    d_ssm = ssm_norm_g.shape[0]
    conv_dim = conv_w.shape[1]
    n_ssm_heads = dt_bias.shape[0]
    n_fox_heads = fox_b_f.shape[0]
    d_fox = n_fox_heads * FOX_HEAD_DIM
    z, xbc, q, k, dtf, vt = _hyb_inproj(xt, g_pre, w_in, d_ssm, conv_dim, n_ssm_heads,
                                        d_fox, n_fox_heads)
    r3 = lambda a: a.reshape(b, s, a.shape[-1])
    y, cum, cumt = _ssd(r3(xbc), r3(z), r3(dtf), conv_w, conv_b, dt_bias, a_log, d_skip,
                        ssm_norm_g, fox_b_f)
    o = _fox(r3(q), r3(k), vt, cum, cumt, n_ssm_heads)
    return _hyb_outproj(y.reshape(b * s, d_ssm), o.reshape(b * s, d_fox), xt, w_out, g_post)


def _rec_body(x_ref, gpre_ref, win_ref, convw_ref, convb_ref, wax_ref, ba_ref, bx_ref, lam_ref,
              wout_ref, gpost_ref, o_ref, buf_ref, a_ref, b_ref, h_ref, *, tt):
    d_rnn = lam_ref.shape[1]
    n_blocks = d_rnn // RNN_BLOCK

    @pl.when(pl.program_id(1) == 0)
    def _():
        buf_ref[0:V7X_SUBLANES, :] = jnp.zeros((V7X_SUBLANES, d_rnn), F32)
        h_ref[...] = jnp.zeros(h_ref.shape, F32)

    x = x_ref[0]
    xn = _rms(x, gpre_ref[...]).astype(BF16)
    gate = _gelu_tanh(_dot(xn, win_ref[:, :d_rnn]))
    xr = _dot(xn, win_ref[:, d_rnn:])

    xc = _causal_conv(buf_ref, xr, convw_ref, convb_ref, tt)

    xcb = xc.astype(BF16)
    r_parts, i_parts = [], []
    for n in range(n_blocks):
        ri = _dot(xcb[:, n * RNN_BLOCK:(n + 1) * RNN_BLOCK], wax_ref[n])
        r_parts.append(ri[:, :RNN_BLOCK])
        i_parts.append(ri[:, RNN_BLOCK:])
    r = _sigmoid(jnp.concatenate(r_parts, axis=1) + ba_ref[...])
    ig = _sigmoid(jnp.concatenate(i_parts, axis=1) + bx_ref[...])
    log_a = RG_LRU_C * r * (-_softplus(-lam_ref[...]))
    a = jnp.exp(log_a)
    u = jnp.sqrt(jnp.tanh(-log_a) * (1.0 + a * a)) * (ig * xc)

    groups = (tt // V7X_SUBLANES, V7X_SUBLANES, d_rnn)
    a = a.reshape(groups)
    u = u.reshape(groups)
    rmod = lax.broadcasted_iota(jnp.int32, groups, 1)
    for sh in (1, 2, 4):
        m = rmod >= sh
        a_sh = jnp.where(m, pltpu.roll(a, sh, 1), 1.0)
        u_sh = jnp.where(m, pltpu.roll(u, sh, 1), 0.0)
        u = a * u_sh + u
        a = a * a_sh
    a_ref[...] = a.reshape(tt, d_rnn)
    b_ref[...] = u.reshape(tt, d_rnn)

    def group(gi, h):
        off = pl.multiple_of(gi * V7X_SUBLANES, V7X_SUBLANES)
        hr = a_ref[pl.ds(off, V7X_SUBLANES), :] * h + b_ref[pl.ds(off, V7X_SUBLANES), :]
        b_ref[pl.ds(off, V7X_SUBLANES), :] = hr
        return hr[V7X_SUBLANES - 1:V7X_SUBLANES, :]

    h_ref[...] = lax.fori_loop(0, tt // V7X_SUBLANES, group, h_ref[...])
    y = (b_ref[...] * gate).astype(BF16)
    o_ref[0] = x + _rms(_dot(y, wout_ref[...]), gpost_ref[...])


def _rec_layer(xt, b, s, g_pre, g_post, w_in, conv_w, conv_b, w_a, b_a, w_x, b_x, lam, w_out):
    d = xt.shape[1]
    d_rnn = lam.shape[0]
    tt = min(REC_TILE, s)
    n_blocks = w_a.shape[0]
    w_ax = jnp.concatenate([w_a, w_x], axis=2).astype(BF16)
    row1 = lambda a: a.reshape(1, -1)
    blk = pl.BlockSpec((1, tt, d), lambda bi, ti: (bi, ti, 0))
    out = pl.pallas_call(
        functools.partial(_rec_body, tt=tt),
        grid=(b, s // tt),
        in_specs=[blk, _const_spec((1, d)), _const_spec((d, 2 * d_rnn)),
                  _const_spec((CONV_K, d_rnn)), _const_spec((1, d_rnn)),
                  _const_spec((n_blocks, RNN_BLOCK, 2 * RNN_BLOCK)),
                  _const_spec((1, d_rnn)), _const_spec((1, d_rnn)), _const_spec((1, d_rnn)),
                  _const_spec((d_rnn, d)), _const_spec((1, d))],
        out_specs=blk,
        out_shape=jax.ShapeDtypeStruct((b, s, d), F32),
        scratch_shapes=[pltpu.VMEM((tt + V7X_SUBLANES, d_rnn), F32),
                        pltpu.VMEM((tt, d_rnn), F32), pltpu.VMEM((tt, d_rnn), F32),
                        pltpu.VMEM((1, d_rnn), F32)],
        compiler_params=_params("parallel", "arbitrary"),
        name="rec",
    )(xt.reshape(b, s, d), row1(g_pre), w_in.astype(BF16), conv_w, row1(conv_b), w_ax,
      row1(b_a), row1(b_x), row1(lam), w_out.astype(BF16), row1(g_post))
    return out.reshape(b * s, d)


def kernel(x, norm_g, ffn_w_in, ffn_w_out, hyb_w_in, ssm_conv_w, ssm_conv_b, ssm_dt_bias, ssm_a_log, ssm_d, ssm_norm_g, fox_b_f, hyb_w_out, rec_w_in, rec_conv_w, rec_conv_b, rec_w_a, rec_b_a, rec_w_x, rec_b_x, rec_lambda, rec_w_out):
    b, s, d = x.shape
    depth = norm_g.shape[0]
    xt = x.reshape(b * s, d)
    ffn_w_in = ffn_w_in.astype(BF16)
    ffn_w_out = ffn_w_out.astype(BF16)
    for layer in range(depth):
        g = norm_g[layer]
        xt = _ffn(xt, g[0], g[1], ffn_w_in[layer, 0], ffn_w_out[layer, 0])
        if layer % 2 == 0:
            i = layer // 2
            xt = _hybrid_layer(xt, b, s, g[2], g[3], hyb_w_in[i], ssm_conv_w[i], ssm_conv_b[i],
                               ssm_dt_bias[i], ssm_a_log[i], ssm_d[i], ssm_norm_g[i],
                               fox_b_f[i], hyb_w_out[i])
        else:
            j = layer // 2
            xt = _rec_layer(xt, b, s, g[2], g[3], rec_w_in[j], rec_conv_w[j], rec_conv_b[j],
                            rec_w_a[j], rec_b_a[j], rec_w_x[j], rec_b_x[j], rec_lambda[j],
                            rec_w_out[j])
        xt = _ffn(xt, g[4], g[5], ffn_w_in[layer, 1], ffn_w_out[layer, 1])
    return xt.reshape(b, s, d)
```

```python
import functools

import jax
import jax.numpy as jnp
from jax import lax
from jax.experimental import pallas as pl
from jax.experimental.pallas import tpu as pltpu

F32 = jnp.float32
BF16 = jnp.bfloat16

NORM_EPS = 1e-6
CONV_K = 4
SSM_HEAD_DIM = 64
SSM_GROUPS = 2
SSM_STATE = 128
SSD_CHUNK = 128
FOX_HEAD_DIM = 128
RG_LRU_C = 8.0
RNN_BLOCK = 128
LOG2E = 1.4426950408889634

V7X_LANES = 128
V7X_SUBLANES = 8
V7X_BF16_SUBLANES = 16
VMEM_LIMIT_BYTES = 56 * 1024 * 1024

TOKEN_TILE = 512
FFN_CHUNK = 256
FFN_SUBTILES = 2
OUTPROJ_SUBTILES = 4
FOX_QUERY_BLOCK = 512
REC_SUBTILES = 2


def _params(*semantics):
    return pltpu.CompilerParams(dimension_semantics=semantics,
                                vmem_limit_bytes=VMEM_LIMIT_BYTES)


def _const_spec(shape):
    zeros = (0,) * len(shape)
    return pl.BlockSpec(shape, lambda *_: zeros)


def _rms(x, g):
    return x * lax.rsqrt(jnp.mean(x * x, axis=-1, keepdims=True) + NORM_EPS) * g


def _sigmoid(x):
    return 0.5 * jnp.tanh(0.5 * x) + 0.5


def _silu(x):
    return x * _sigmoid(x)


def _gelu_tanh(x):
    inner = x * (0.7978845608028654 + 0.035677408136300125 * (x * x))
    return x * (0.5 * jnp.tanh(inner) + 0.5)


def _softplus(x):
    return jnp.maximum(x, 0.0) + jnp.log1p(jnp.exp(-jnp.abs(x)))


def _dot(a, b):
    return jnp.dot(a, b, preferred_element_type=F32)


def _split(x, terms):
    parts = []
    for _ in range(terms - 1):
        part = x.astype(BF16)
        parts.append(part)
        x = x - part.astype(F32)
    parts.append(x.astype(BF16))
    return parts


def _dot_exact_lhs(m, x, terms=3):
    return sum(_dot(m, part) for part in _split(x, terms))


def _dot_exact_rhs(x, m, terms=3):
    return sum(_dot(part, m) for part in _split(x, terms))


def _causal_conv(buf_ref, x, w_ref, b_ref, offset=0, last=True):
    rows = x.shape[0]
    base = V7X_SUBLANES + offset
    buf_ref[base:base + rows, :] = x
    w = w_ref[...]
    y = b_ref[...] + w[CONV_K - 1:CONV_K] * x
    for k in range(CONV_K - 1):
        y = y + w[k:k + 1] * buf_ref[pl.ds(base - (CONV_K - 1) + k, rows), :]
    if last:
        buf_ref[0:V7X_SUBLANES, :] = buf_ref[offset + rows:base + rows, :]
    return y


def _ffn_body(x_ref, gpre_ref, gpost_ref, win_ref, wout_ref, o_ref, h_ref, *, fc, n_chunks, sub):
    dff = fc * n_chunks
    for r in range(x_ref.shape[0] // sub):
        rows = slice(r * sub, (r + 1) * sub)
        x = x_ref[rows, :]
        xn = _rms(x, gpre_ref[...]).astype(BF16)
        for c in range(n_chunks):
            gate = _dot(xn, win_ref[:, fc * c:fc * (c + 1)])
            up = _dot(xn, win_ref[:, dff + fc * c:dff + fc * (c + 1)])
            h_ref[rows, fc * c:fc * (c + 1)] = (_silu(gate) * up).astype(BF16)
        y = _dot(h_ref[rows, :], wout_ref[...])
        o_ref[rows, :] = x + 0.5 * _rms(y, gpost_ref[...])


def _ffn(xt, g_pre, g_post, w_in, w_out):
    t, d = xt.shape
    dff = w_out.shape[0]
    fc = FFN_CHUNK
    n_chunks = dff // fc
    assert n_chunks * fc == dff
    sub = min(TOKEN_TILE, t)
    tm = min(FFN_SUBTILES * sub, t)
    resident = lambda shape: pl.BlockSpec(shape, lambda i: (0, 0), pipeline_mode=pl.Buffered(1))
    return pl.pallas_call(
        functools.partial(_ffn_body, fc=fc, n_chunks=n_chunks, sub=sub),
        grid=(t // tm,),
        in_specs=[pl.BlockSpec((tm, d), lambda i: (i, 0)),
                  _const_spec((1, d)), _const_spec((1, d)),
                  resident((d, 2 * dff)), resident((dff, d))],
        out_specs=pl.BlockSpec((tm, d), lambda i: (i, 0)),
        out_shape=jax.ShapeDtypeStruct((t, d), F32),
        scratch_shapes=[pltpu.VMEM((tm, dff), BF16)],
        compiler_params=_params("parallel"),
        name="ffn",
    )(xt, g_pre.reshape(1, d), g_post.reshape(1, d), w_in, w_out)


def _inproj_body(x_ref, g_ref, w_ref, wvt_ref, *o_refs, splits):
    xn = _rms(x_ref[...], g_ref[...]).astype(BF16)
    for o_ref, (start, width, mult) in zip(o_refs[:-1], splits):
        y = _dot(xn, w_ref[:, start:start + width])
        o_ref[...] = (y if mult is None else y * mult).astype(o_ref.dtype)
    vt = lax.dot_general(wvt_ref[...], xn, (((1,), (1,)), ((), ())), preferred_element_type=F32)
    o_refs[-1][...] = vt.astype(o_refs[-1].dtype)


def _hyb_inproj(xt, g, w_in, d_ssm, conv_dim, n_ssm_heads, d_fox, n_fox_heads):
    t, d = xt.shape
    tm = min(TOKEN_TILE, t)
    o_xbc = d_ssm
    o_dt = o_xbc + conv_dim
    o_q = o_dt + n_ssm_heads
    o_v = o_q + 2 * d_fox
    o_f = o_v + d_fox
    pad = V7X_LANES - n_ssm_heads - n_fox_heads
    w_perm = jnp.concatenate(
        [w_in[:, :o_dt], w_in[:, o_q:o_v], w_in[:, o_dt:o_q], w_in[:, o_f:],
         jnp.zeros((d, pad), w_in.dtype)], axis=1).astype(BF16)
    w_vt = w_in[:, o_v:o_f].T.astype(BF16)
    widths = (d_ssm, conv_dim, d_fox, d_fox, V7X_LANES)
    dtypes = (F32, F32, BF16, BF16, F32)
    mults = (None, None, FOX_HEAD_DIM ** -0.5 * LOG2E, None, None)
    splits, start = [], 0
    for w, mult in zip(widths, mults):
        splits.append((start, w, mult))
        start += w
    return pl.pallas_call(
        functools.partial(_inproj_body, splits=tuple(splits)),
        grid=(t // tm,),
        in_specs=[pl.BlockSpec((tm, d), lambda i: (i, 0)), _const_spec((1, d)),
                  _const_spec((d, start)), _const_spec((d_fox, d))],
        out_specs=[pl.BlockSpec((tm, w), lambda i: (i, 0)) for w in widths]
        + [pl.BlockSpec((d_fox, tm), lambda i: (0, i))],
        out_shape=[jax.ShapeDtypeStruct((t, w), dt) for w, dt in zip(widths, dtypes)]
        + [jax.ShapeDtypeStruct((d_fox, t), BF16)],
        compiler_params=_params("parallel"),
        name="hyb_inproj",
    )(xt, g.reshape(1, d), w_perm, w_vt)


def _ssd_body(xbc_ref, z_ref, dtf_ref, convw_ref, convb_ref, bias_ref, arow_ref, dskip_ref,
              ng_ref, tri_ref, expand_ref,
              y_ref, cum_ref, cumt_ref,
              buf_ref, state_ref, carry_ref, carryt_ref, *, n_heads, n_fox_heads):
    L = SSD_CHUNK
    hd = SSM_HEAD_DIM
    ns = SSM_STATE
    d_ssm = n_heads * hd
    gw = d_ssm // SSM_GROUPS
    heads_per_group = n_heads // SSM_GROUPS

    @pl.when(pl.program_id(1) == 0)
    def _():
        buf_ref[0:V7X_SUBLANES, :] = jnp.zeros((V7X_SUBLANES, buf_ref.shape[1]), F32)
        state_ref[...] = jnp.zeros(state_ref.shape, F32)
        carry_ref[...] = jnp.zeros(carry_ref.shape, F32)
        carryt_ref[...] = jnp.zeros(carryt_ref.shape, F32)

    act = _silu(_causal_conv(buf_ref, xbc_ref[0], convw_ref, convb_ref))
    xs = act[:, :d_ssm]
    bm = act[:, d_ssm:d_ssm + SSM_GROUPS * ns]
    cm = act[:, d_ssm + SSM_GROUPS * ns:]

    lane = lax.broadcasted_iota(jnp.int32, (L, V7X_LANES), 1)
    is_dt = lane < n_heads
    v = dtf_ref[0] + bias_ref[...]
    sp = _softplus(jnp.where(is_dt, v, -v))
    val = jnp.where(is_dt, sp, -sp)
    da = val * arow_ref[...]
    cs = _dot_exact_lhs(tri_ref[...], da) * LOG2E
    cst = cs.T

    expand = expand_ref[...]
    xdt = xs * _dot_exact_rhs(val, expand, terms=2)
    cs_x = _dot_exact_rhs(cs, expand)
    exp_a_x = jnp.exp2(cs_x)
    decay_x = jnp.exp2(cs_x[L - 1:L, :] - cs_x)

    row_i = lax.broadcasted_iota(jnp.int32, (L, L), 0)
    col_i = lax.broadcasted_iota(jnp.int32, (L, L), 1)
    causal = row_i >= col_i
    first_half = lane < hd

    y_parts = []
    for g in range(SSM_GROUPS):
        bg = bm[:, g * ns:(g + 1) * ns]
        cg = cm[:, g * ns:(g + 1) * ns].astype(BF16)
        cb = lax.dot_general(cg, bg.astype(BF16), (((1,), (1,)), ((), ())),
                             preferred_element_type=F32)
        gs = slice(g * gw, (g + 1) * gw)
        s_prev = state_ref[g]
        y_off = _dot(cg, s_prev.astype(BF16)) * exp_a_x[:, gs]
        contrib = _dot(bg.T.astype(BF16), (xdt[:, gs] * decay_x[:, gs]).astype(BF16))
        state_ref[g] = s_prev * exp_a_x[L - 1:L, gs] + contrib
        for pr in range(heads_per_group // 2):
            c0 = g * gw + pr * 2 * hd
            x_pair = xdt[:, c0:c0 + 2 * hd]
            acc = None
            for j in range(2):
                h = g * heads_per_group + 2 * pr + j
                seg = cs[:, h:h + 1] - cst[h:h + 1, :]
                lmat = jnp.exp2(jnp.where(causal, seg, -jnp.inf))
                m = (lmat * cb).astype(BF16)
                keep = first_half if j == 0 else jnp.logical_not(first_half)
                part = _dot(m, jnp.where(keep, x_pair, 0.0).astype(BF16))
                acc = part if acc is None else acc + part
            y_parts.append(acc + y_off[:, c0 - g * gw:c0 - g * gw + 2 * hd])
    y = jnp.concatenate(y_parts, axis=1) + dskip_ref[...] * xs
    y = y * _silu(z_ref[0])
    normed = []
    for g in range(SSM_GROUPS):
        yg = y[:, g * gw:(g + 1) * gw]
        normed.append(yg * lax.rsqrt(jnp.mean(yg * yg, axis=-1, keepdims=True) + NORM_EPS))
    y_ref[0] = (jnp.concatenate(normed, axis=1) * ng_ref[...]).astype(y_ref.dtype)

    cum = cs + carry_ref[...]
    cum_ref[0] = cum
    carry_ref[...] = cum[L - 1:L, :]
    cumt = cst + carryt_ref[...]
    cumt_ref[0] = cumt[n_heads:n_heads + n_fox_heads, :]
    carryt_ref[...] = jnp.broadcast_to(cumt[:, L - 1:L], carryt_ref.shape)


def _ssd(xbc, z, dtf, conv_w, conv_b, dt_bias, a_log, d_skip, norm_g, fox_b_f):
    b, s, conv_dim = xbc.shape
    d_ssm = z.shape[-1]
    n_heads = dt_bias.shape[0]
    n_fox = fox_b_f.shape[0]
    L = SSD_CHUNK
    pad = V7X_LANES - n_heads - n_fox
    bias_row = jnp.concatenate([dt_bias, fox_b_f, jnp.zeros((pad,), F32)]).reshape(1, V7X_LANES)
    a_row = jnp.concatenate([-jnp.exp(a_log), jnp.ones((n_fox,), F32),
                             jnp.zeros((pad,), F32)]).reshape(1, V7X_LANES)
    d_row = jnp.repeat(d_skip, SSM_HEAD_DIM).reshape(1, d_ssm)
    tri = (jnp.arange(L)[:, None] >= jnp.arange(L)[None, :]).astype(BF16)
    expand = (jnp.arange(V7X_LANES)[:, None] == (jnp.arange(d_ssm)[None, :] // SSM_HEAD_DIM)).astype(BF16)
    blk = lambda w: pl.BlockSpec((1, L, w), lambda i, c: (i, c, 0))
    return pl.pallas_call(
        functools.partial(_ssd_body, n_heads=n_heads, n_fox_heads=n_fox),
        grid=(b, s // L),
        in_specs=[blk(conv_dim), blk(d_ssm), blk(V7X_LANES),
                  _const_spec((CONV_K, conv_dim)), _const_spec((1, conv_dim)),
                  _const_spec((1, V7X_LANES)), _const_spec((1, V7X_LANES)),
                  _const_spec((1, d_ssm)), _const_spec((1, d_ssm)),
                  _const_spec((L, L)), _const_spec((V7X_LANES, d_ssm))],
        out_specs=[blk(d_ssm), blk(V7X_LANES),
                   pl.BlockSpec((1, n_fox, L), lambda i, c: (i, 0, c))],
        out_shape=[jax.ShapeDtypeStruct((b, s, d_ssm), BF16),
                   jax.ShapeDtypeStruct((b, s, V7X_LANES), F32),
                   jax.ShapeDtypeStruct((b, n_fox, s), F32)],
        scratch_shapes=[pltpu.VMEM((L + V7X_SUBLANES, conv_dim), F32),
                        pltpu.VMEM((SSM_GROUPS, SSM_STATE, d_ssm // SSM_GROUPS), F32),
                        pltpu.VMEM((1, V7X_LANES), F32),
                        pltpu.VMEM((V7X_LANES, L), F32)],
        compiler_params=_params("parallel", "arbitrary"),
        name="ssd",
    )(xbc, z, dtf, conv_w, conv_b.reshape(1, conv_dim), bias_row, a_row, d_row,
      norm_g.reshape(1, d_ssm), tri, expand)


def _fox_body(q_ref, k_ref, vt_ref, cum_ref, cumt_ref, o_ref, s_ref, bm_ref, m_ref, l_ref, acc_ref, *,
              n_heads, qb, kb, cum_col0):
    i = pl.program_id(1)
    hd = FOX_HEAD_DIM
    heads = range(n_heads)
    key_i = lax.broadcasted_iota(jnp.int32, (kb, qb), 0)
    qry_i = lax.broadcasted_iota(jnp.int32, (kb, qb), 1)
    m_ref[...] = jnp.full(m_ref.shape, -jnp.inf, F32)
    l_ref[...] = jnp.zeros(l_ref.shape, F32)
    acc_ref[...] = jnp.zeros(acc_ref.shape, F32)
    cq2 = cumt_ref[0]
    ones_rows = jnp.ones((V7X_BF16_SUBLANES, kb), BF16)

    def scores(slot, j, diag_offset=None):
        start = pl.multiple_of(j * kb, kb)
        for h in heads:
            hs = slice(h * hd, (h + 1) * hd)
            k = k_ref[0, pl.ds(start, kb), hs]
            ck2 = cum_ref[0, pl.ds(start, kb), cum_col0 + h:cum_col0 + h + 1]
            s = lax.dot_general(k, q_ref[0, :, hs], (((1,), (1,)), ((), ())),
                                preferred_element_type=F32) - ck2
            if diag_offset is not None:
                s = jnp.where(key_i + diag_offset <= qry_i, s, -jnp.inf)
            s_ref[slot, h] = s
            bm_ref[slot, h, 0:1, :] = jnp.max(s, axis=0, keepdims=True)

    def softmax_pv(slot, j):
        start = pl.multiple_of(j * kb, kb)
        for h in heads:
            cq2h = cq2[h:h + 1]
            m_old = m_ref[h, 0:1, :]
            m_new = jnp.maximum(m_old, cq2h + bm_ref[slot, h, 0:1, :])
            alpha = jnp.exp2(m_old - m_new)
            p = jnp.exp2(s_ref[slot, h] - (m_new - cq2h)).astype(BF16)
            m_ref[h, 0:1, :] = m_new
            vt = jnp.concatenate([vt_ref[h * hd:(h + 1) * hd, pl.ds(start, kb)], ones_rows], axis=0)
            pv = _dot(vt, p)
            acc_ref[h] = alpha * acc_ref[h] + pv[:hd]
            l_ref[h, 0:1, :] = alpha * l_ref[h, 0:1, :] + pv[hd:hd + 1]

    scores(0, 2 * i, diag_offset=0)
    scores(1, 2 * i + 1, diag_offset=kb)
    softmax_pv(0, 2 * i)

    def pair(u, carry):
        scores(0, 2 * u)
        softmax_pv(1, jnp.where(u == 0, 2 * i + 1, 2 * u - 1))
        scores(1, 2 * u + 1)
        softmax_pv(0, 2 * u)
        return carry

    lax.fori_loop(0, i, pair, 0)
    softmax_pv(1, jnp.where(i == 0, 1, 2 * i - 1))

    for h in heads:
        o = acc_ref[h] / l_ref[h, 0:1, :]
        o_ref[0, :, h * hd:(h + 1) * hd] = o.T.astype(o_ref.dtype)


def _fox(q, k, vt, cum, cumt, n_ssm_heads):
    b, s, d_fox = q.shape
    n_heads = cumt.shape[1]
    qb = min(FOX_QUERY_BLOCK, s)
    kb = qb // 2
    assert s % qb == 0
    qspec = pl.BlockSpec((1, qb, d_fox), lambda bi, i: (bi, i, 0))
    return pl.pallas_call(
        functools.partial(_fox_body, n_heads=n_heads, qb=qb, kb=kb, cum_col0=n_ssm_heads),
        grid=(b, s // qb),
        in_specs=[qspec,
                  pl.BlockSpec((1, s, d_fox), lambda bi, i: (bi, 0, 0)),
                  pl.BlockSpec((d_fox, s), lambda bi, i: (0, bi)),
                  pl.BlockSpec((1, s, V7X_LANES), lambda bi, i: (bi, 0, 0)),
                  pl.BlockSpec((1, n_heads, qb), lambda bi, i: (bi, 0, i))],
        out_specs=qspec,
        out_shape=jax.ShapeDtypeStruct((b, s, d_fox), BF16),
        scratch_shapes=[pltpu.VMEM((2, n_heads, kb, qb), F32),
                        pltpu.VMEM((2, n_heads, V7X_SUBLANES, qb), F32),
                        pltpu.VMEM((n_heads, V7X_SUBLANES, qb), F32),
                        pltpu.VMEM((n_heads, V7X_SUBLANES, qb), F32),
                        pltpu.VMEM((n_heads, FOX_HEAD_DIM, qb), F32)],
        compiler_params=_params("parallel", "arbitrary"),
        name="fox",
    )(q, k, vt, cum, cumt)


def _outproj_body(y_ref, o_ref, x_ref, w_ref, g_ref, out_ref, *, sub):
    d_y = y_ref.shape[1]
    for r in range(x_ref.shape[0] // sub):
        rows = slice(r * sub, (r + 1) * sub)
        h = _dot(y_ref[rows, :], w_ref[:d_y, :]) + _dot(o_ref[rows, :], w_ref[d_y:, :])
        out_ref[rows, :] = x_ref[rows, :] + _rms(h, g_ref[...])


def _hyb_outproj(y, o, xt, w_out, g):
    t, d = xt.shape
    sub = min(TOKEN_TILE, t)
    tm = min(OUTPROJ_SUBTILES * sub, t)
    d_y, d_o = y.shape[1], o.shape[1]
    row = lambda w: pl.BlockSpec((tm, w), lambda i: (i, 0))
    return pl.pallas_call(
        functools.partial(_outproj_body, sub=sub),
        grid=(t // tm,),
        in_specs=[row(d_y), row(d_o), row(d), _const_spec((d_y + d_o, d)), _const_spec((1, d))],
        out_specs=row(d),
        out_shape=jax.ShapeDtypeStruct((t, d), F32),
        compiler_params=_params("parallel"),
        name="hyb_outproj",
    )(y, o, xt, w_out.astype(BF16), g.reshape(1, d))


def _hybrid_layer(xt, b, s, g_pre, g_post, w_in, conv_w, conv_b, dt_bias, a_log, d_skip,
                  ssm_norm_g, fox_b_f, w_out):
    d_ssm = ssm_norm_g.shape[0]
    conv_dim = conv_w.shape[1]
    n_ssm_heads = dt_bias.shape[0]
    n_fox_heads = fox_b_f.shape[0]
    d_fox = n_fox_heads * FOX_HEAD_DIM
    z, xbc, q, k, dtf, vt = _hyb_inproj(xt, g_pre, w_in, d_ssm, conv_dim, n_ssm_heads,
                                        d_fox, n_fox_heads)
    r3 = lambda a: a.reshape(b, s, a.shape[-1])
    y, cum, cumt = _ssd(r3(xbc), r3(z), r3(dtf), conv_w, conv_b, dt_bias, a_log, d_skip,
                        ssm_norm_g, fox_b_f)
    o = _fox(r3(q), r3(k), vt, cum, cumt, n_ssm_heads)
    return _hyb_outproj(y.reshape(b * s, d_ssm), o.reshape(b * s, d_fox), xt, w_out, g_post)


def _rec_body(x_ref, gpre_ref, win_ref, convw_ref, convb_ref, wax_ref, ba_ref, bx_ref, lam_ref,
              wout_ref, gpost_ref, o_ref, buf_ref, a_ref, b_ref, h_ref, *, tt, sub):
    d_rnn = lam_ref.shape[1]
    n_blocks = d_rnn // RNN_BLOCK

    @pl.when(pl.program_id(1) == 0)
    def _():
        buf_ref[0:V7X_SUBLANES, :] = jnp.zeros((V7X_SUBLANES, d_rnn), F32)
        h_ref[...] = jnp.zeros(h_ref.shape, F32)

    log_sig_lam = -_softplus(-lam_ref[...])
    n_sub = tt // sub
    xs, gates = [], []
    for r in range(n_sub):
        rows = slice(r * sub, (r + 1) * sub)
        x = x_ref[0, rows, :]
        xn = _rms(x, gpre_ref[...]).astype(BF16)
        gates.append(_gelu_tanh(_dot(xn, win_ref[:, :d_rnn])))
        xr = _dot(xn, win_ref[:, d_rnn:])
        xs.append(x)

        xc = _causal_conv(buf_ref, xr, convw_ref, convb_ref, offset=r * sub, last=r == n_sub - 1)

        xcb = xc.astype(BF16)
        r_parts, i_parts = [], []
        for n in range(n_blocks):
            ri = _dot(xcb[:, n * RNN_BLOCK:(n + 1) * RNN_BLOCK], wax_ref[n])
            r_parts.append(ri[:, :RNN_BLOCK])
            i_parts.append(ri[:, RNN_BLOCK:])
        rg = _sigmoid(jnp.concatenate(r_parts, axis=1) + ba_ref[...])
        ig = _sigmoid(jnp.concatenate(i_parts, axis=1) + bx_ref[...])
        log_a = RG_LRU_C * rg * log_sig_lam
        a = jnp.exp(log_a)
        u = jnp.sqrt(jnp.tanh(-log_a) * (1.0 + a * a)) * (ig * xc)

        groups = (sub // V7X_SUBLANES, V7X_SUBLANES, d_rnn)
        a = a.reshape(groups)
        u = u.reshape(groups)
        rmod = lax.broadcasted_iota(jnp.int32, groups, 1)
        for sh in (1, 2, 4):
            m = rmod >= sh
            a_sh = jnp.where(m, pltpu.roll(a, sh, 1), 1.0)
            u_sh = jnp.where(m, pltpu.roll(u, sh, 1), 0.0)
            u = a * u_sh + u
            a = a * a_sh
        a_ref[rows, :] = a.reshape(sub, d_rnn)
        b_ref[rows, :] = u.reshape(sub, d_rnn)

    def group(gi, h):
        off = pl.multiple_of(gi * V7X_SUBLANES, V7X_SUBLANES)
        hr = a_ref[pl.ds(off, V7X_SUBLANES), :] * h + b_ref[pl.ds(off, V7X_SUBLANES), :]
        b_ref[pl.ds(off, V7X_SUBLANES), :] = hr
        return hr[V7X_SUBLANES - 1:V7X_SUBLANES, :]

    h_ref[...] = lax.fori_loop(0, tt // V7X_SUBLANES, group, h_ref[...])
    for r in range(n_sub):
        rows = slice(r * sub, (r + 1) * sub)
        y = (b_ref[rows, :] * gates[r]).astype(BF16)
        o_ref[0, rows, :] = xs[r] + _rms(_dot(y, wout_ref[...]), gpost_ref[...])


def _rec_layer(xt, b, s, g_pre, g_post, w_in, conv_w, conv_b, w_a, b_a, w_x, b_x, lam, w_out):
    d = xt.shape[1]
    d_rnn = lam.shape[0]
    sub = min(TOKEN_TILE, s)
    tt = min(REC_SUBTILES * sub, s)
    n_blocks = w_a.shape[0]
    w_ax = jnp.concatenate([w_a, w_x], axis=2).astype(BF16)
    row1 = lambda a: a.reshape(1, -1)
    blk = pl.BlockSpec((1, tt, d), lambda bi, ti: (bi, ti, 0))
    out = pl.pallas_call(
        functools.partial(_rec_body, tt=tt, sub=sub),
        grid=(b, s // tt),
        in_specs=[blk, _const_spec((1, d)), _const_spec((d, 2 * d_rnn)),
                  _const_spec((CONV_K, d_rnn)), _const_spec((1, d_rnn)),
                  _const_spec((n_blocks, RNN_BLOCK, 2 * RNN_BLOCK)),
                  _const_spec((1, d_rnn)), _const_spec((1, d_rnn)), _const_spec((1, d_rnn)),
                  _const_spec((d_rnn, d)), _const_spec((1, d))],
        out_specs=blk,
        out_shape=jax.ShapeDtypeStruct((b, s, d), F32),
        scratch_shapes=[pltpu.VMEM((tt + V7X_SUBLANES, d_rnn), F32),
                        pltpu.VMEM((tt, d_rnn), F32), pltpu.VMEM((tt, d_rnn), F32),
                        pltpu.VMEM((1, d_rnn), F32)],
        compiler_params=_params("parallel", "arbitrary"),
        name="rec",
    )(xt.reshape(b, s, d), row1(g_pre), w_in.astype(BF16), conv_w, row1(conv_b), w_ax,
      row1(b_a), row1(b_x), row1(lam), w_out.astype(BF16), row1(g_post))
    return out.reshape(b * s, d)


def kernel(x, norm_g, ffn_w_in, ffn_w_out, hyb_w_in, ssm_conv_w, ssm_conv_b, ssm_dt_bias, ssm_a_log, ssm_d, ssm_norm_g, fox_b_f, hyb_w_out, rec_w_in, rec_conv_w, rec_conv_b, rec_w_a, rec_b_a, rec_w_x, rec_b_x, rec_lambda, rec_w_out):
    b, s, d = x.shape
    depth = norm_g.shape[0]
    xt = x.reshape(b * s, d)
    ffn_w_in = ffn_w_in.astype(BF16)
    ffn_w_out = ffn_w_out.astype(BF16)
    for layer in range(depth):
        g = norm_g[layer]
        xt = _ffn(xt, g[0], g[1], ffn_w_in[layer, 0], ffn_w_out[layer, 0])
        if layer % 2 == 0:
            i = layer // 2
            xt = _hybrid_layer(xt, b, s, g[2], g[3], hyb_w_in[i], ssm_conv_w[i], ssm_conv_b[i],
                               ssm_dt_bias[i], ssm_a_log[i], ssm_d[i], ssm_norm_g[i],
                               fox_b_f[i], hyb_w_out[i])
        else:
            j = layer // 2
            xt = _rec_layer(xt, b, s, g[2], g[3], rec_w_in[j], rec_conv_w[j], rec_conv_b[j],
                            rec_w_a[j], rec_b_a[j], rec_w_x[j], rec_b_x[j], rec_lambda[j],
                            rec_w_out[j])
        xt = _ffn(xt, g[4], g[5], ffn_w_in[layer, 1], ffn_w_out[layer, 1])
    return xt.reshape(b, s, d)
```

```python
import functools

import jax
import jax.numpy as jnp
from jax import lax
from jax.experimental import pallas as pl
from jax.experimental.pallas import tpu as pltpu

F32 = jnp.float32
BF16 = jnp.bfloat16

NORM_EPS = 1e-6
CONV_K = 4
SSM_HEAD_DIM = 64
SSM_GROUPS = 2
SSM_STATE = 128
SSD_CHUNK = 128
FOX_HEAD_DIM = 128
RG_LRU_C = 8.0
RNN_BLOCK = 128
LOG2E = 1.4426950408889634

V7X_LANES = 128
V7X_SUBLANES = 8
V7X_BF16_SUBLANES = 16
VMEM_LIMIT_BYTES = 56 * 1024 * 1024

TOKEN_TILE = 512
FFN_CHUNK = 256
FFN_SUBTILES = 2
OUTPROJ_SUBTILES = 4
FOX_BLOCK = 512
REC_SUBTILES = 2


def _params(*semantics):
    return pltpu.CompilerParams(dimension_semantics=semantics,
                                vmem_limit_bytes=VMEM_LIMIT_BYTES)


def _const_spec(shape):
    zeros = (0,) * len(shape)
    return pl.BlockSpec(shape, lambda *_: zeros)


def _rms(x, g):
    return x * lax.rsqrt(jnp.mean(x * x, axis=-1, keepdims=True) + NORM_EPS) * g


def _sigmoid(x):
    return 0.5 * jnp.tanh(0.5 * x) + 0.5


def _silu(x):
    return x * _sigmoid(x)


def _gelu_tanh(x):
    inner = x * (0.7978845608028654 + 0.035677408136300125 * (x * x))
    return x * (0.5 * jnp.tanh(inner) + 0.5)


def _softplus(x):
    return jnp.maximum(x, 0.0) + jnp.log1p(jnp.exp(-jnp.abs(x)))


def _dot(a, b):
    return jnp.dot(a, b, preferred_element_type=F32)


def _split(x, terms):
    parts = []
    for _ in range(terms - 1):
        part = x.astype(BF16)
        parts.append(part)
        x = x - part.astype(F32)
    parts.append(x.astype(BF16))
    return parts


def _dot_exact_lhs(m, x, terms=3):
    return sum(_dot(m, part) for part in _split(x, terms))


def _dot_exact_rhs(x, m, terms=3):
    return sum(_dot(part, m) for part in _split(x, terms))


def _causal_conv(buf_ref, x, w_ref, b_ref, offset=0, last=True):
    rows = x.shape[0]
    base = V7X_SUBLANES + offset
    buf_ref[base:base + rows, :] = x
    w = w_ref[...]
    y = b_ref[...] + w[CONV_K - 1:CONV_K] * x
    for k in range(CONV_K - 1):
        y = y + w[k:k + 1] * buf_ref[pl.ds(base - (CONV_K - 1) + k, rows), :]
    if last:
        buf_ref[0:V7X_SUBLANES, :] = buf_ref[offset + rows:base + rows, :]
    return y


def _ffn_body(x_ref, gpre_ref, gpost_ref, win_ref, wout_ref, o_ref, h_ref, *, fc, n_chunks, sub):
    dff = fc * n_chunks
    for r in range(x_ref.shape[0] // sub):
        rows = slice(r * sub, (r + 1) * sub)
        x = x_ref[rows, :]
        xn = _rms(x, gpre_ref[...]).astype(BF16)
        for c in range(n_chunks):
            gate = _dot(xn, win_ref[:, fc * c:fc * (c + 1)])
            up = _dot(xn, win_ref[:, dff + fc * c:dff + fc * (c + 1)])
            h_ref[rows, fc * c:fc * (c + 1)] = (_silu(gate) * up).astype(BF16)
        y = _dot(h_ref[rows, :], wout_ref[...])
        o_ref[rows, :] = x + 0.5 * _rms(y, gpost_ref[...])


def _ffn(xt, g_pre, g_post, w_in, w_out):
    t, d = xt.shape
    dff = w_out.shape[0]
    fc = FFN_CHUNK
    n_chunks = dff // fc
    assert n_chunks * fc == dff
    sub = min(TOKEN_TILE, t)
    tm = min(FFN_SUBTILES * sub, t)
    resident = lambda shape: pl.BlockSpec(shape, lambda i: (0, 0), pipeline_mode=pl.Buffered(1))
    return pl.pallas_call(
        functools.partial(_ffn_body, fc=fc, n_chunks=n_chunks, sub=sub),
        grid=(t // tm,),
        in_specs=[pl.BlockSpec((tm, d), lambda i: (i, 0)),
                  _const_spec((1, d)), _const_spec((1, d)),
                  resident((d, 2 * dff)), resident((dff, d))],
        out_specs=pl.BlockSpec((tm, d), lambda i: (i, 0)),
        out_shape=jax.ShapeDtypeStruct((t, d), F32),
        scratch_shapes=[pltpu.VMEM((tm, dff), BF16)],
        compiler_params=_params("parallel"),
        name="ffn",
    )(xt, g_pre.reshape(1, d), g_post.reshape(1, d), w_in, w_out)


def _inproj_body(x_ref, g_ref, w_ref, wvt_ref, *o_refs, splits):
    xn = _rms(x_ref[...], g_ref[...]).astype(BF16)
    for o_ref, (start, width, mult) in zip(o_refs[:-1], splits):
        y = _dot(xn, w_ref[:, start:start + width])
        o_ref[...] = (y if mult is None else y * mult).astype(o_ref.dtype)
    vt = lax.dot_general(wvt_ref[...], xn, (((1,), (1,)), ((), ())), preferred_element_type=F32)
    o_refs[-1][...] = vt.astype(o_refs[-1].dtype)


def _hyb_inproj(xt, g, w_in, d_ssm, conv_dim, n_ssm_heads, d_fox, n_fox_heads):
    t, d = xt.shape
    tm = min(TOKEN_TILE, t)
    o_xbc = d_ssm
    o_dt = o_xbc + conv_dim
    o_q = o_dt + n_ssm_heads
    o_v = o_q + 2 * d_fox
    o_f = o_v + d_fox
    pad = V7X_LANES - n_ssm_heads - n_fox_heads
    w_perm = jnp.concatenate(
        [w_in[:, :o_dt], w_in[:, o_q:o_v], w_in[:, o_dt:o_q], w_in[:, o_f:],
         jnp.zeros((d, pad), w_in.dtype)], axis=1).astype(BF16)
    w_vt = w_in[:, o_v:o_f].T.astype(BF16)
    widths = (d_ssm, conv_dim, d_fox, d_fox, V7X_LANES)
    dtypes = (F32, F32, BF16, BF16, F32)
    mults = (None, None, FOX_HEAD_DIM ** -0.5 * LOG2E, None, None)
    splits, start = [], 0
    for w, mult in zip(widths, mults):
        splits.append((start, w, mult))
        start += w
    return pl.pallas_call(
        functools.partial(_inproj_body, splits=tuple(splits)),
        grid=(t // tm,),
        in_specs=[pl.BlockSpec((tm, d), lambda i: (i, 0)), _const_spec((1, d)),
                  _const_spec((d, start)), _const_spec((d_fox, d))],
        out_specs=[pl.BlockSpec((tm, w), lambda i: (i, 0)) for w in widths]
        + [pl.BlockSpec((d_fox, tm), lambda i: (0, i))],
        out_shape=[jax.ShapeDtypeStruct((t, w), dt) for w, dt in zip(widths, dtypes)]
        + [jax.ShapeDtypeStruct((d_fox, t), BF16)],
        compiler_params=_params("parallel"),
        name="hyb_inproj",
    )(xt, g.reshape(1, d), w_perm, w_vt)


def _ssd_body(xbc_ref, z_ref, dtf_ref, convw_ref, convb_ref, bias_ref, arow_ref, dskip_ref,
              ng_ref, tri_ref, expand_ref,
              y_ref, cum_ref, cumt_ref,
              buf_ref, state_ref, carry_ref, carryt_ref, *, n_heads, n_fox_heads):
    L = SSD_CHUNK
    hd = SSM_HEAD_DIM
    ns = SSM_STATE
    d_ssm = n_heads * hd
    gw = d_ssm // SSM_GROUPS
    heads_per_group = n_heads // SSM_GROUPS

    @pl.when(pl.program_id(1) == 0)
    def _():
        buf_ref[0:V7X_SUBLANES, :] = jnp.zeros((V7X_SUBLANES, buf_ref.shape[1]), F32)
        state_ref[...] = jnp.zeros(state_ref.shape, F32)
        carry_ref[...] = jnp.zeros(carry_ref.shape, F32)
        carryt_ref[...] = jnp.zeros(carryt_ref.shape, F32)

    act = _silu(_causal_conv(buf_ref, xbc_ref[0], convw_ref, convb_ref))
    xs = act[:, :d_ssm]
    bm = act[:, d_ssm:d_ssm + SSM_GROUPS * ns]
    cm = act[:, d_ssm + SSM_GROUPS * ns:]

    lane = lax.broadcasted_iota(jnp.int32, (L, V7X_LANES), 1)
    is_dt = lane < n_heads
    v = dtf_ref[0] + bias_ref[...]
    sp = _softplus(jnp.where(is_dt, v, -v))
    val = jnp.where(is_dt, sp, -sp)
    da = val * arow_ref[...]
    cs = _dot_exact_lhs(tri_ref[...], da) * LOG2E
    cst = cs.T

    expand = expand_ref[...]
    xdt = xs * _dot_exact_rhs(val, expand, terms=2)
    cs_x = _dot_exact_rhs(cs, expand)
    exp_a_x = jnp.exp2(cs_x)
    decay_x = jnp.exp2(cs_x[L - 1:L, :] - cs_x)

    row_i = lax.broadcasted_iota(jnp.int32, (L, L), 0)
    col_i = lax.broadcasted_iota(jnp.int32, (L, L), 1)
    causal = row_i >= col_i
    first_half = lane < hd

    y_parts = []
    for g in range(SSM_GROUPS):
        bg = bm[:, g * ns:(g + 1) * ns]
        cg = cm[:, g * ns:(g + 1) * ns].astype(BF16)
        cb = lax.dot_general(cg, bg.astype(BF16), (((1,), (1,)), ((), ())),
                             preferred_element_type=F32)
        gs = slice(g * gw, (g + 1) * gw)
        s_prev = state_ref[g]
        y_off = _dot(cg, s_prev.astype(BF16)) * exp_a_x[:, gs]
        contrib = _dot(bg.T.astype(BF16), (xdt[:, gs] * decay_x[:, gs]).astype(BF16))
        state_ref[g] = s_prev * exp_a_x[L - 1:L, gs] + contrib
        for pr in range(heads_per_group // 2):
            c0 = g * gw + pr * 2 * hd
            x_pair = xdt[:, c0:c0 + 2 * hd]
            acc = None
            for j in range(2):
                h = g * heads_per_group + 2 * pr + j
                seg = cs[:, h:h + 1] - cst[h:h + 1, :]
                lmat = jnp.exp2(jnp.where(causal, seg, -jnp.inf))
                m = (lmat * cb).astype(BF16)
                keep = first_half if j == 0 else jnp.logical_not(first_half)
                part = _dot(m, jnp.where(keep, x_pair, 0.0).astype(BF16))
                acc = part if acc is None else acc + part
            y_parts.append(acc + y_off[:, c0 - g * gw:c0 - g * gw + 2 * hd])
    y = jnp.concatenate(y_parts, axis=1) + dskip_ref[...] * xs
    y = y * _silu(z_ref[0])
    normed = []
    for g in range(SSM_GROUPS):
        yg = y[:, g * gw:(g + 1) * gw]
        normed.append(yg * lax.rsqrt(jnp.mean(yg * yg, axis=-1, keepdims=True) + NORM_EPS))
    y_ref[0] = (jnp.concatenate(normed, axis=1) * ng_ref[...]).astype(y_ref.dtype)

    cum = cs + carry_ref[...]
    cum_ref[0] = cum
    carry_ref[...] = cum[L - 1:L, :]
    cumt = cst + carryt_ref[...]
    cumt_ref[0] = cumt[n_heads:n_heads + n_fox_heads, :]
    carryt_ref[...] = jnp.broadcast_to(cumt[:, L - 1:L], carryt_ref.shape)


def _ssd(xbc, z, dtf, conv_w, conv_b, dt_bias, a_log, d_skip, norm_g, fox_b_f):
    b, s, conv_dim = xbc.shape
    d_ssm = z.shape[-1]
    n_heads = dt_bias.shape[0]
    n_fox = fox_b_f.shape[0]
    L = SSD_CHUNK
    pad = V7X_LANES - n_heads - n_fox
    bias_row = jnp.concatenate([dt_bias, fox_b_f, jnp.zeros((pad,), F32)]).reshape(1, V7X_LANES)
    a_row = jnp.concatenate([-jnp.exp(a_log), jnp.ones((n_fox,), F32),
                             jnp.zeros((pad,), F32)]).reshape(1, V7X_LANES)
    d_row = jnp.repeat(d_skip, SSM_HEAD_DIM).reshape(1, d_ssm)
    tri = (jnp.arange(L)[:, None] >= jnp.arange(L)[None, :]).astype(BF16)
    expand = (jnp.arange(V7X_LANES)[:, None] == (jnp.arange(d_ssm)[None, :] // SSM_HEAD_DIM)).astype(BF16)
    blk = lambda w: pl.BlockSpec((1, L, w), lambda i, c: (i, c, 0))
    return pl.pallas_call(
        functools.partial(_ssd_body, n_heads=n_heads, n_fox_heads=n_fox),
        grid=(b, s // L),
        in_specs=[blk(conv_dim), blk(d_ssm), blk(V7X_LANES),
                  _const_spec((CONV_K, conv_dim)), _const_spec((1, conv_dim)),
                  _const_spec((1, V7X_LANES)), _const_spec((1, V7X_LANES)),
                  _const_spec((1, d_ssm)), _const_spec((1, d_ssm)),
                  _const_spec((L, L)), _const_spec((V7X_LANES, d_ssm))],
        out_specs=[blk(d_ssm), blk(V7X_LANES),
                   pl.BlockSpec((1, n_fox, L), lambda i, c: (i, 0, c))],
        out_shape=[jax.ShapeDtypeStruct((b, s, d_ssm), BF16),
                   jax.ShapeDtypeStruct((b, s, V7X_LANES), F32),
                   jax.ShapeDtypeStruct((b, n_fox, s), F32)],
        scratch_shapes=[pltpu.VMEM((L + V7X_SUBLANES, conv_dim), F32),
                        pltpu.VMEM((SSM_GROUPS, SSM_STATE, d_ssm // SSM_GROUPS), F32),
                        pltpu.VMEM((1, V7X_LANES), F32),
                        pltpu.VMEM((V7X_LANES, L), F32)],
        compiler_params=_params("parallel", "arbitrary"),
        name="ssd",
    )(xbc, z, dtf, conv_w, conv_b.reshape(1, conv_dim), bias_row, a_row, d_row,
      norm_g.reshape(1, d_ssm), tri, expand)


def _fox_body(q_ref, k_ref, vt_ref, cum_ref, cumt_ref, o_ref, s_ref, bm_ref, m_ref, l_ref, acc_ref, *,
              n_heads, blk, cum_col0):
    i = pl.program_id(1)
    hd = FOX_HEAD_DIM
    half = n_heads // 2
    groups = (range(half), range(half, n_heads))
    key_i = lax.broadcasted_iota(jnp.int32, (blk, blk), 0)
    qry_i = lax.broadcasted_iota(jnp.int32, (blk, blk), 1)
    causal = key_i <= qry_i
    m_ref[...] = jnp.full(m_ref.shape, -jnp.inf, F32)
    l_ref[...] = jnp.zeros(l_ref.shape, F32)
    acc_ref[...] = jnp.zeros(acc_ref.shape, F32)
    cq2 = cumt_ref[0]
    ones_rows = jnp.ones((V7X_BF16_SUBLANES, blk), BF16)

    def scores(g, j, masked=False):
        start = pl.multiple_of(j * blk, blk)
        for h in groups[g]:
            hs = slice(h * hd, (h + 1) * hd)
            k = k_ref[0, pl.ds(start, blk), hs]
            ck2 = cum_ref[0, pl.ds(start, blk), cum_col0 + h:cum_col0 + h + 1]
            s = lax.dot_general(k, q_ref[0, :, hs], (((1,), (1,)), ((), ())),
                                preferred_element_type=F32) - ck2
            if masked:
                s = jnp.where(causal, s, -jnp.inf)
            s_ref[h] = s
            bm_ref[h, 0:1, :] = jnp.max(s, axis=0, keepdims=True)

    def softmax_pv(g, j):
        start = pl.multiple_of(j * blk, blk)
        for h in groups[g]:
            cq2h = cq2[h:h + 1]
            m_old = m_ref[h, 0:1, :]
            m_new = jnp.maximum(m_old, cq2h + bm_ref[h, 0:1, :])
            alpha = jnp.exp2(m_old - m_new)
            p = jnp.exp2(s_ref[h] - (m_new - cq2h)).astype(BF16)
            m_ref[h, 0:1, :] = m_new
            vt = jnp.concatenate([vt_ref[h * hd:(h + 1) * hd, pl.ds(start, blk)], ones_rows], axis=0)
            pv = _dot(vt, p)
            acc_ref[h] = alpha * acc_ref[h] + pv[:hd]
            l_ref[h, 0:1, :] = alpha * l_ref[h, 0:1, :] + pv[hd:hd + 1]

    scores(0, i, masked=True)
    scores(1, i, masked=True)
    softmax_pv(0, i)

    def step(j, carry):
        scores(0, j)
        softmax_pv(1, jnp.where(j == 0, i, j - 1))
        scores(1, j)
        softmax_pv(0, j)
        return carry

    lax.fori_loop(0, i, step, 0)
    softmax_pv(1, jnp.where(i == 0, i, i - 1))

    for h in range(n_heads):
        o = acc_ref[h] / l_ref[h, 0:1, :]
        o_ref[0, :, h * hd:(h + 1) * hd] = o.T.astype(o_ref.dtype)


def _fox(q, k, vt, cum, cumt, n_ssm_heads):
    b, s, d_fox = q.shape
    n_heads = cumt.shape[1]
    blk = min(FOX_BLOCK, s)
    assert s % blk == 0 and n_heads % 2 == 0
    qspec = pl.BlockSpec((1, blk, d_fox), lambda bi, i: (bi, i, 0))
    return pl.pallas_call(
        functools.partial(_fox_body, n_heads=n_heads, blk=blk, cum_col0=n_ssm_heads),
        grid=(b, s // blk),
        in_specs=[qspec,
                  pl.BlockSpec((1, s, d_fox), lambda bi, i: (bi, 0, 0)),
                  pl.BlockSpec((d_fox, s), lambda bi, i: (0, bi)),
                  pl.BlockSpec((1, s, V7X_LANES), lambda bi, i: (bi, 0, 0)),
                  pl.BlockSpec((1, n_heads, blk), lambda bi, i: (bi, 0, i))],
        out_specs=qspec,
        out_shape=jax.ShapeDtypeStruct((b, s, d_fox), BF16),
        scratch_shapes=[pltpu.VMEM((n_heads, blk, blk), F32),
                        pltpu.VMEM((n_heads, V7X_SUBLANES, blk), F32),
                        pltpu.VMEM((n_heads, V7X_SUBLANES, blk), F32),
                        pltpu.VMEM((n_heads, V7X_SUBLANES, blk), F32),
                        pltpu.VMEM((n_heads, FOX_HEAD_DIM, blk), F32)],
        compiler_params=_params("parallel", "arbitrary"),
        name="fox",
    )(q, k, vt, cum, cumt)


def _outproj_body(y_ref, o_ref, x_ref, w_ref, g_ref, out_ref, *, sub):
    d_y = y_ref.shape[1]
    for r in range(x_ref.shape[0] // sub):
        rows = slice(r * sub, (r + 1) * sub)
        h = _dot(y_ref[rows, :], w_ref[:d_y, :]) + _dot(o_ref[rows, :], w_ref[d_y:, :])
        out_ref[rows, :] = x_ref[rows, :] + _rms(h, g_ref[...])


def _hyb_outproj(y, o, xt, w_out, g):
    t, d = xt.shape
    sub = min(TOKEN_TILE, t)
    tm = min(OUTPROJ_SUBTILES * sub, t)
    d_y, d_o = y.shape[1], o.shape[1]
    row = lambda w: pl.BlockSpec((tm, w), lambda i: (i, 0))
    return pl.pallas_call(
        functools.partial(_outproj_body, sub=sub),
        grid=(t // tm,),
        in_specs=[row(d_y), row(d_o), row(d), _const_spec((d_y + d_o, d)), _const_spec((1, d))],
        out_specs=row(d),
        out_shape=jax.ShapeDtypeStruct((t, d), F32),
        compiler_params=_params("parallel"),
        name="hyb_outproj",
    )(y, o, xt, w_out.astype(BF16), g.reshape(1, d))


def _hybrid_layer(xt, b, s, g_pre, g_post, w_in, conv_w, conv_b, dt_bias, a_log, d_skip,
                  ssm_norm_g, fox_b_f, w_out):
    d_ssm = ssm_norm_g.shape[0]
    conv_dim = conv_w.shape[1]
    n_ssm_heads = dt_bias.shape[0]
    n_fox_heads = fox_b_f.shape[0]
    d_fox = n_fox_heads * FOX_HEAD_DIM
    z, xbc, q, k, dtf, vt = _hyb_inproj(xt, g_pre, w_in, d_ssm, conv_dim, n_ssm_heads,
                                        d_fox, n_fox_heads)
    r3 = lambda a: a.reshape(b, s, a.shape[-1])
    y, cum, cumt = _ssd(r3(xbc), r3(z), r3(dtf), conv_w, conv_b, dt_bias, a_log, d_skip,
                        ssm_norm_g, fox_b_f)
    o = _fox(r3(q), r3(k), vt, cum, cumt, n_ssm_heads)
    return _hyb_outproj(y.reshape(b * s, d_ssm), o.reshape(b * s, d_fox), xt, w_out, g_post)


def _rec_body(x_ref, gpre_ref, win_ref, convw_ref, convb_ref, wax_ref, ba_ref, bx_ref, lam_ref,
              wout_ref, gpost_ref, o_ref, buf_ref, a_ref, b_ref, h_ref, *, tt, sub):
    d_rnn = lam_ref.shape[1]
    n_blocks = d_rnn // RNN_BLOCK

    @pl.when(pl.program_id(1) == 0)
    def _():
        buf_ref[0:V7X_SUBLANES, :] = jnp.zeros((V7X_SUBLANES, d_rnn), F32)
        h_ref[...] = jnp.zeros(h_ref.shape, F32)

    log_sig_lam = -_softplus(-lam_ref[...])
    n_sub = tt // sub
    xs, gates = [], []
    for r in range(n_sub):
        rows = slice(r * sub, (r + 1) * sub)
        x = x_ref[0, rows, :]
        xn = _rms(x, gpre_ref[...]).astype(BF16)
        gates.append(_gelu_tanh(_dot(xn, win_ref[:, :d_rnn])))
        xr = _dot(xn, win_ref[:, d_rnn:])
        xs.append(x)

        xc = _causal_conv(buf_ref, xr, convw_ref, convb_ref, offset=r * sub, last=r == n_sub - 1)

        xcb = xc.astype(BF16)
        r_parts, i_parts = [], []
        for n in range(n_blocks):
            ri = _dot(xcb[:, n * RNN_BLOCK:(n + 1) * RNN_BLOCK], wax_ref[n])
            r_parts.append(ri[:, :RNN_BLOCK])
            i_parts.append(ri[:, RNN_BLOCK:])
        rg = _sigmoid(jnp.concatenate(r_parts, axis=1) + ba_ref[...])
        ig = _sigmoid(jnp.concatenate(i_parts, axis=1) + bx_ref[...])
        log_a = RG_LRU_C * rg * log_sig_lam
        a = jnp.exp(log_a)
        u = jnp.sqrt(jnp.tanh(-log_a) * (1.0 + a * a)) * (ig * xc)

        groups = (sub // V7X_SUBLANES, V7X_SUBLANES, d_rnn)
        a = a.reshape(groups)
        u = u.reshape(groups)
        rmod = lax.broadcasted_iota(jnp.int32, groups, 1)
        for sh in (1, 2, 4):
            m = rmod >= sh
            a_sh = jnp.where(m, pltpu.roll(a, sh, 1), 1.0)
            u_sh = jnp.where(m, pltpu.roll(u, sh, 1), 0.0)
            u = a * u_sh + u
            a = a * a_sh
        a_ref[rows, :] = a.reshape(sub, d_rnn)
        b_ref[rows, :] = u.reshape(sub, d_rnn)

    def group(gi, h):
        off = pl.multiple_of(gi * V7X_SUBLANES, V7X_SUBLANES)
        hr = a_ref[pl.ds(off, V7X_SUBLANES), :] * h + b_ref[pl.ds(off, V7X_SUBLANES), :]
        b_ref[pl.ds(off, V7X_SUBLANES), :] = hr
        return hr[V7X_SUBLANES - 1:V7X_SUBLANES, :]

    h_ref[...] = lax.fori_loop(0, tt // V7X_SUBLANES, group, h_ref[...])
    for r in range(n_sub):
        rows = slice(r * sub, (r + 1) * sub)
        y = (b_ref[rows, :] * gates[r]).astype(BF16)
        o_ref[0, rows, :] = xs[r] + _rms(_dot(y, wout_ref[...]), gpost_ref[...])


def _rec_layer(xt, b, s, g_pre, g_post, w_in, conv_w, conv_b, w_a, b_a, w_x, b_x, lam, w_out):
    d = xt.shape[1]
    d_rnn = lam.shape[0]
    sub = min(TOKEN_TILE, s)
    tt = min(REC_SUBTILES * sub, s)
    n_blocks = w_a.shape[0]
    w_ax = jnp.concatenate([w_a, w_x], axis=2).astype(BF16)
    row1 = lambda a: a.reshape(1, -1)
    blk = pl.BlockSpec((1, tt, d), lambda bi, ti: (bi, ti, 0))
    out = pl.pallas_call(
        functools.partial(_rec_body, tt=tt, sub=sub),
        grid=(b, s // tt),
        in_specs=[blk, _const_spec((1, d)), _const_spec((d, 2 * d_rnn)),
                  _const_spec((CONV_K, d_rnn)), _const_spec((1, d_rnn)),
                  _const_spec((n_blocks, RNN_BLOCK, 2 * RNN_BLOCK)),
                  _const_spec((1, d_rnn)), _const_spec((1, d_rnn)), _const_spec((1, d_rnn)),
                  _const_spec((d_rnn, d)), _const_spec((1, d))],
        out_specs=blk,
        out_shape=jax.ShapeDtypeStruct((b, s, d), F32),
        scratch_shapes=[pltpu.VMEM((tt + V7X_SUBLANES, d_rnn), F32),
                        pltpu.VMEM((tt, d_rnn), F32), pltpu.VMEM((tt, d_rnn), F32),
                        pltpu.VMEM((1, d_rnn), F32)],
        compiler_params=_params("parallel", "arbitrary"),
        name="rec",
    )(xt.reshape(b, s, d), row1(g_pre), w_in.astype(BF16), conv_w, row1(conv_b), w_ax,
      row1(b_a), row1(b_x), row1(lam), w_out.astype(BF16), row1(g_post))
    return out.reshape(b * s, d)


def kernel(x, norm_g, ffn_w_in, ffn_w_out, hyb_w_in, ssm_conv_w, ssm_conv_b, ssm_dt_bias, ssm_a_log, ssm_d, ssm_norm_g, fox_b_f, hyb_w_out, rec_w_in, rec_conv_w, rec_conv_b, rec_w_a, rec_b_a, rec_w_x, rec_b_x, rec_lambda, rec_w_out):
    b, s, d = x.shape
    depth = norm_g.shape[0]
    xt = x.reshape(b * s, d)
    ffn_w_in = ffn_w_in.astype(BF16)
    ffn_w_out = ffn_w_out.astype(BF16)
    for layer in range(depth):
        g = norm_g[layer]
        xt = _ffn(xt, g[0], g[1], ffn_w_in[layer, 0], ffn_w_out[layer, 0])
        if layer % 2 == 0:
            i = layer // 2
            xt = _hybrid_layer(xt, b, s, g[2], g[3], hyb_w_in[i], ssm_conv_w[i], ssm_conv_b[i],
                               ssm_dt_bias[i], ssm_a_log[i], ssm_d[i], ssm_norm_g[i],
                               fox_b_f[i], hyb_w_out[i])
        else:
            j = layer // 2
            xt = _rec_layer(xt, b, s, g[2], g[3], rec_w_in[j], rec_conv_w[j], rec_conv_b[j],
                            rec_w_a[j], rec_b_a[j], rec_w_x[j], rec_b_x[j], rec_lambda[j],
                            rec_w_out[j])
        xt = _ffn(xt, g[4], g[5], ffn_w_in[layer, 1], ffn_w_out[layer, 1])
    return xt.reshape(b, s, d)
```

```python
import functools

import jax
import jax.numpy as jnp
from jax import lax
from jax.experimental import pallas as pl
from jax.experimental.pallas import tpu as pltpu

F32 = jnp.float32
BF16 = jnp.bfloat16

NORM_EPS = 1e-6
CONV_K = 4
SSM_HEAD_DIM = 64
SSM_GROUPS = 2
SSM_STATE = 128
SSD_CHUNK = 128
FOX_HEAD_DIM = 128
RG_LRU_C = 8.0
RNN_BLOCK = 128
LOG2E = 1.4426950408889634

V7X_LANES = 128
V7X_SUBLANES = 8
V7X_BF16_SUBLANES = 16
VMEM_LIMIT_BYTES = 56 * 1024 * 1024

TOKEN_TILE = 512
FFN_CHUNK = 256
FFN_SUBTILES = 2
OUTPROJ_SUBTILES = 4
FOX_BLOCK = 512
REC_SUBTILES = 2


def _params(*semantics):
    return pltpu.CompilerParams(dimension_semantics=semantics,
                                vmem_limit_bytes=VMEM_LIMIT_BYTES)


def _const_spec(shape):
    zeros = (0,) * len(shape)
    return pl.BlockSpec(shape, lambda *_: zeros)


def _rms(x, g):
    return x * lax.rsqrt(jnp.mean(x * x, axis=-1, keepdims=True) + NORM_EPS) * g


def _sigmoid(x):
    return 0.5 * jnp.tanh(0.5 * x) + 0.5


def _silu(x):
    return x * _sigmoid(x)


def _gelu_tanh(x):
    inner = x * (0.7978845608028654 + 0.035677408136300125 * (x * x))
    return x * (0.5 * jnp.tanh(inner) + 0.5)


def _softplus(x):
    return jnp.maximum(x, 0.0) + jnp.log1p(jnp.exp(-jnp.abs(x)))


def _dot(a, b):
    return jnp.dot(a, b, preferred_element_type=F32)


def _split(x, terms):
    parts = []
    for _ in range(terms - 1):
        part = x.astype(BF16)
        parts.append(part)
        x = x - part.astype(F32)
    parts.append(x.astype(BF16))
    return parts


def _dot_exact_lhs(m, x, terms=3):
    return sum(_dot(m, part) for part in _split(x, terms))


def _dot_exact_rhs(x, m, terms=3):
    return sum(_dot(part, m) for part in _split(x, terms))


def _causal_conv(buf_ref, x, w_ref, b_ref, offset=0, last=True):
    rows = x.shape[0]
    base = V7X_SUBLANES + offset
    buf_ref[base:base + rows, :] = x
    w = w_ref[...]
    y = b_ref[...] + w[CONV_K - 1:CONV_K] * x
    for k in range(CONV_K - 1):
        y = y + w[k:k + 1] * buf_ref[pl.ds(base - (CONV_K - 1) + k, rows), :]
    if last:
        buf_ref[0:V7X_SUBLANES, :] = buf_ref[offset + rows:base + rows, :]
    return y


def _ffn_body(x_ref, gpre_ref, gpost_ref, win_ref, wout_ref, o_ref, h_ref, *, fc, n_chunks, sub):
    dff = fc * n_chunks
    for r in range(x_ref.shape[0] // sub):
        rows = slice(r * sub, (r + 1) * sub)
        x = x_ref[rows, :]
        xn = _rms(x, gpre_ref[...]).astype(BF16)
        for c in range(n_chunks):
            gate = _dot(xn, win_ref[:, fc * c:fc * (c + 1)])
            up = _dot(xn, win_ref[:, dff + fc * c:dff + fc * (c + 1)])
            h_ref[rows, fc * c:fc * (c + 1)] = (_silu(gate) * up).astype(BF16)
        y = _dot(h_ref[rows, :], wout_ref[...])
        o_ref[rows, :] = x + 0.5 * _rms(y, gpost_ref[...])


def _ffn(xt, g_pre, g_post, w_in, w_out):
    t, d = xt.shape
    dff = w_out.shape[0]
    fc = FFN_CHUNK
    n_chunks = dff // fc
    assert n_chunks * fc == dff
    sub = min(TOKEN_TILE, t)
    tm = min(FFN_SUBTILES * sub, t)
    resident = lambda shape: pl.BlockSpec(shape, lambda i: (0, 0), pipeline_mode=pl.Buffered(1))
    return pl.pallas_call(
        functools.partial(_ffn_body, fc=fc, n_chunks=n_chunks, sub=sub),
        grid=(t // tm,),
        in_specs=[pl.BlockSpec((tm, d), lambda i: (i, 0)),
                  _const_spec((1, d)), _const_spec((1, d)),
                  resident((d, 2 * dff)), resident((dff, d))],
        out_specs=pl.BlockSpec((tm, d), lambda i: (i, 0)),
        out_shape=jax.ShapeDtypeStruct((t, d), F32),
        scratch_shapes=[pltpu.VMEM((tm, dff), BF16)],
        compiler_params=_params("parallel"),
        name="ffn",
    )(xt, g_pre.reshape(1, d), g_post.reshape(1, d), w_in, w_out)


def _inproj_body(x_ref, g_ref, w_ref, wvt_ref, *o_refs, splits):
    xn = _rms(x_ref[...], g_ref[...]).astype(BF16)
    for o_ref, (start, width, mult) in zip(o_refs[:-1], splits):
        y = _dot(xn, w_ref[:, start:start + width])
        o_ref[...] = (y if mult is None else y * mult).astype(o_ref.dtype)
    vt = lax.dot_general(wvt_ref[...], xn, (((1,), (1,)), ((), ())), preferred_element_type=F32)
    o_refs[-1][...] = vt.astype(o_refs[-1].dtype)


def _hyb_inproj(xt, g, w_in, d_ssm, conv_dim, n_ssm_heads, d_fox, n_fox_heads):
    t, d = xt.shape
    tm = min(TOKEN_TILE, t)
    o_xbc = d_ssm
    o_dt = o_xbc + conv_dim
    o_q = o_dt + n_ssm_heads
    o_v = o_q + 2 * d_fox
    o_f = o_v + d_fox
    pad = V7X_LANES - n_ssm_heads - n_fox_heads
    w_perm = jnp.concatenate(
        [w_in[:, :o_dt], w_in[:, o_q:o_v], w_in[:, o_dt:o_q], w_in[:, o_f:],
         jnp.zeros((d, pad), w_in.dtype)], axis=1).astype(BF16)
    w_vt = w_in[:, o_v:o_f].T.astype(BF16)
    widths = (d_ssm, conv_dim, d_fox, d_fox, V7X_LANES)
    dtypes = (F32, F32, BF16, BF16, F32)
    mults = (None, None, FOX_HEAD_DIM ** -0.5 * LOG2E, None, None)
    splits, start = [], 0
    for w, mult in zip(widths, mults):
        splits.append((start, w, mult))
        start += w
    return pl.pallas_call(
        functools.partial(_inproj_body, splits=tuple(splits)),
        grid=(t // tm,),
        in_specs=[pl.BlockSpec((tm, d), lambda i: (i, 0)), _const_spec((1, d)),
                  _const_spec((d, start)), _const_spec((d_fox, d))],
        out_specs=[pl.BlockSpec((tm, w), lambda i: (i, 0)) for w in widths]
        + [pl.BlockSpec((d_fox, tm), lambda i: (0, i))],
        out_shape=[jax.ShapeDtypeStruct((t, w), dt) for w, dt in zip(widths, dtypes)]
        + [jax.ShapeDtypeStruct((d_fox, t), BF16)],
        compiler_params=_params("parallel"),
        name="hyb_inproj",
    )(xt, g.reshape(1, d), w_perm, w_vt)


def _ssd_body(xbc_ref, z_ref, dtf_ref, convw_ref, convb_ref, bias_ref, arow_ref, dskip_ref,
              ng_ref, tri_ref, expand_ref,
              y_ref, cum_ref, cumt_ref,
              buf_ref, state_ref, carry_ref, carryt_ref, *, n_heads, n_fox_heads):
    L = SSD_CHUNK
    hd = SSM_HEAD_DIM
    ns = SSM_STATE
    d_ssm = n_heads * hd
    gw = d_ssm // SSM_GROUPS
    heads_per_group = n_heads // SSM_GROUPS

    @pl.when(pl.program_id(1) == 0)
    def _():
        buf_ref[0:V7X_SUBLANES, :] = jnp.zeros((V7X_SUBLANES, buf_ref.shape[1]), F32)
        state_ref[...] = jnp.zeros(state_ref.shape, F32)
        carry_ref[...] = jnp.zeros(carry_ref.shape, F32)
        carryt_ref[...] = jnp.zeros(carryt_ref.shape, F32)

    act = _silu(_causal_conv(buf_ref, xbc_ref[0], convw_ref, convb_ref))
    xs = act[:, :d_ssm]
    bm = act[:, d_ssm:d_ssm + SSM_GROUPS * ns]
    cm = act[:, d_ssm + SSM_GROUPS * ns:]

    lane = lax.broadcasted_iota(jnp.int32, (L, V7X_LANES), 1)
    is_dt = lane < n_heads
    v = dtf_ref[0] + bias_ref[...]
    sp = _softplus(jnp.where(is_dt, v, -v))
    val = jnp.where(is_dt, sp, -sp)
    da = val * arow_ref[...]
    cs = _dot_exact_lhs(tri_ref[...], da) * LOG2E
    cst = cs.T

    expand = expand_ref[...]
    xdt = xs * _dot_exact_rhs(val, expand, terms=2)
    cs_x = _dot_exact_rhs(cs, expand)
    exp_a_x = jnp.exp2(cs_x)
    decay_x = jnp.exp2(cs_x[L - 1:L, :] - cs_x)

    row_i = lax.broadcasted_iota(jnp.int32, (L, L), 0)
    col_i = lax.broadcasted_iota(jnp.int32, (L, L), 1)
    causal = row_i >= col_i
    first_half = lane < hd

    y_parts = []
    for g in range(SSM_GROUPS):
        bg = bm[:, g * ns:(g + 1) * ns]
        cg = cm[:, g * ns:(g + 1) * ns].astype(BF16)
        cb = lax.dot_general(cg, bg.astype(BF16), (((1,), (1,)), ((), ())),
                             preferred_element_type=F32)
        gs = slice(g * gw, (g + 1) * gw)
        s_prev = state_ref[g]
        y_off = _dot(cg, s_prev.astype(BF16)) * exp_a_x[:, gs]
        contrib = _dot(bg.T.astype(BF16), (xdt[:, gs] * decay_x[:, gs]).astype(BF16))
        state_ref[g] = s_prev * exp_a_x[L - 1:L, gs] + contrib
        for pr in range(heads_per_group // 2):
            c0 = g * gw + pr * 2 * hd
            x_pair = xdt[:, c0:c0 + 2 * hd]
            acc = None
            for j in range(2):
                h = g * heads_per_group + 2 * pr + j
                seg = cs[:, h:h + 1] - cst[h:h + 1, :]
                lmat = jnp.exp2(jnp.where(causal, seg, -jnp.inf))
                m = (lmat * cb).astype(BF16)
                keep = first_half if j == 0 else jnp.logical_not(first_half)
                part = _dot(m, jnp.where(keep, x_pair, 0.0).astype(BF16))
                acc = part if acc is None else acc + part
            y_parts.append(acc + y_off[:, c0 - g * gw:c0 - g * gw + 2 * hd])
    y = jnp.concatenate(y_parts, axis=1) + dskip_ref[...] * xs
    y = y * _silu(z_ref[0])
    normed = []
    for g in range(SSM_GROUPS):
        yg = y[:, g * gw:(g + 1) * gw]
        normed.append(yg * lax.rsqrt(jnp.mean(yg * yg, axis=-1, keepdims=True) + NORM_EPS))
    y_ref[0] = (jnp.concatenate(normed, axis=1) * ng_ref[...]).astype(y_ref.dtype)

    cum = cs + carry_ref[...]
    cum_ref[0] = cum
    carry_ref[...] = cum[L - 1:L, :]
    cumt = cst + carryt_ref[...]
    cumt_ref[0] = cumt[n_heads:n_heads + n_fox_heads, :]
    carryt_ref[...] = jnp.broadcast_to(cumt[:, L - 1:L], carryt_ref.shape)


def _ssd(xbc, z, dtf, conv_w, conv_b, dt_bias, a_log, d_skip, norm_g, fox_b_f):
    b, s, conv_dim = xbc.shape
    d_ssm = z.shape[-1]
    n_heads = dt_bias.shape[0]
    n_fox = fox_b_f.shape[0]
    L = SSD_CHUNK
    pad = V7X_LANES - n_heads - n_fox
    bias_row = jnp.concatenate([dt_bias, fox_b_f, jnp.zeros((pad,), F32)]).reshape(1, V7X_LANES)
    a_row = jnp.concatenate([-jnp.exp(a_log), jnp.ones((n_fox,), F32),
                             jnp.zeros((pad,), F32)]).reshape(1, V7X_LANES)
    d_row = jnp.repeat(d_skip, SSM_HEAD_DIM).reshape(1, d_ssm)
    tri = (jnp.arange(L)[:, None] >= jnp.arange(L)[None, :]).astype(BF16)
    expand = (jnp.arange(V7X_LANES)[:, None] == (jnp.arange(d_ssm)[None, :] // SSM_HEAD_DIM)).astype(BF16)
    blk = lambda w: pl.BlockSpec((1, L, w), lambda i, c: (i, c, 0))
    return pl.pallas_call(
        functools.partial(_ssd_body, n_heads=n_heads, n_fox_heads=n_fox),
        grid=(b, s // L),
        in_specs=[blk(conv_dim), blk(d_ssm), blk(V7X_LANES),
                  _const_spec((CONV_K, conv_dim)), _const_spec((1, conv_dim)),
                  _const_spec((1, V7X_LANES)), _const_spec((1, V7X_LANES)),
                  _const_spec((1, d_ssm)), _const_spec((1, d_ssm)),
                  _const_spec((L, L)), _const_spec((V7X_LANES, d_ssm))],
        out_specs=[blk(d_ssm), blk(V7X_LANES),
                   pl.BlockSpec((1, n_fox, L), lambda i, c: (i, 0, c))],
        out_shape=[jax.ShapeDtypeStruct((b, s, d_ssm), BF16),
                   jax.ShapeDtypeStruct((b, s, V7X_LANES), F32),
                   jax.ShapeDtypeStruct((b, n_fox, s), F32)],
        scratch_shapes=[pltpu.VMEM((L + V7X_SUBLANES, conv_dim), F32),
                        pltpu.VMEM((SSM_GROUPS, SSM_STATE, d_ssm // SSM_GROUPS), F32),
                        pltpu.VMEM((1, V7X_LANES), F32),
                        pltpu.VMEM((V7X_LANES, L), F32)],
        compiler_params=_params("parallel", "arbitrary"),
        name="ssd",
    )(xbc, z, dtf, conv_w, conv_b.reshape(1, conv_dim), bias_row, a_row, d_row,
      norm_g.reshape(1, d_ssm), tri, expand)


def _fox_body(q_ref, k_ref, vt_ref, cum_ref, cumt_ref, o_ref, s_ref, bm_ref, m_ref, l_ref, acc_ref, *,
              n_heads, blk, cum_col0):
    i = pl.program_id(1)
    hd = FOX_HEAD_DIM
    half = n_heads // 2
    groups = (range(half), range(half, n_heads))
    key_i = lax.broadcasted_iota(jnp.int32, (blk, blk), 0)
    qry_i = lax.broadcasted_iota(jnp.int32, (blk, blk), 1)
    causal = key_i <= qry_i
    m_ref[...] = jnp.full(m_ref.shape, -jnp.inf, F32)
    l_ref[...] = jnp.zeros(l_ref.shape, F32)
    acc_ref[...] = jnp.zeros(acc_ref.shape, F32)
    cq2 = cumt_ref[0]
    ones_rows = jnp.ones((V7X_BF16_SUBLANES, blk), BF16)

    def scores(g, j, masked=False):
        start = pl.multiple_of(j * blk, blk)
        for h in groups[g]:
            hs = slice(h * hd, (h + 1) * hd)
            k = k_ref[0, pl.ds(start, blk), hs]
            ck2 = cum_ref[0, pl.ds(start, blk), cum_col0 + h:cum_col0 + h + 1]
            s = lax.dot_general(k, q_ref[0, :, hs], (((1,), (1,)), ((), ())),
                                preferred_element_type=F32) - ck2
            if masked:
                s = jnp.where(causal, s, -jnp.inf)
            s_ref[h] = s
            bm_ref[h, 0:1, :] = jnp.max(s, axis=0, keepdims=True)

    def softmax_pv(g, j):
        start = pl.multiple_of(j * blk, blk)
        for h in groups[g]:
            cq2h = cq2[h:h + 1]
            m_old = m_ref[h, 0:1, :]
            m_new = jnp.maximum(m_old, cq2h + bm_ref[h, 0:1, :])
            alpha = jnp.exp2(m_old - m_new)
            p = jnp.exp2(s_ref[h] - (m_new - cq2h)).astype(BF16)
            m_ref[h, 0:1, :] = m_new
            vt = jnp.concatenate([vt_ref[h * hd:(h + 1) * hd, pl.ds(start, blk)], ones_rows], axis=0)
            pv = _dot(vt, p)
            acc_ref[h] = alpha * acc_ref[h] + pv[:hd]
            l_ref[h, 0:1, :] = alpha * l_ref[h, 0:1, :] + pv[hd:hd + 1]

    scores(0, i, masked=True)
    scores(1, i, masked=True)
    softmax_pv(0, i)

    def step(j, carry):
        scores(0, j)
        softmax_pv(1, jnp.where(j == 0, i, j - 1))
        scores(1, j)
        softmax_pv(0, j)
        return carry

    lax.fori_loop(0, i, step, 0)
    softmax_pv(1, jnp.where(i == 0, i, i - 1))

    for h in range(n_heads):
        o = acc_ref[h] * (1.0 / l_ref[h, 0:1, :])
        o_ref[0, :, h * hd:(h + 1) * hd] = o.T.astype(o_ref.dtype)


def _fox(q, k, vt, cum, cumt, n_ssm_heads):
    b, s, d_fox = q.shape
    n_heads = cumt.shape[1]
    blk = min(FOX_BLOCK, s)
    assert s % blk == 0 and n_heads % 2 == 0
    qspec = pl.BlockSpec((1, blk, d_fox), lambda bi, i: (bi, i, 0))
    return pl.pallas_call(
        functools.partial(_fox_body, n_heads=n_heads, blk=blk, cum_col0=n_ssm_heads),
        grid=(b, s // blk),
        in_specs=[qspec,
                  pl.BlockSpec((1, s, d_fox), lambda bi, i: (bi, 0, 0)),
                  pl.BlockSpec((d_fox, s), lambda bi, i: (0, bi)),
                  pl.BlockSpec((1, s, V7X_LANES), lambda bi, i: (bi, 0, 0)),
                  pl.BlockSpec((1, n_heads, blk), lambda bi, i: (bi, 0, i))],
        out_specs=qspec,
        out_shape=jax.ShapeDtypeStruct((b, s, d_fox), BF16),
        scratch_shapes=[pltpu.VMEM((n_heads, blk, blk), F32),
                        pltpu.VMEM((n_heads, V7X_SUBLANES, blk), F32),
                        pltpu.VMEM((n_heads, V7X_SUBLANES, blk), F32),
                        pltpu.VMEM((n_heads, V7X_SUBLANES, blk), F32),
                        pltpu.VMEM((n_heads, FOX_HEAD_DIM, blk), F32)],
        compiler_params=_params("parallel", "arbitrary"),
        name="fox",
    )(q, k, vt, cum, cumt)


def _outproj_body(y_ref, o_ref, x_ref, w_ref, g_ref, out_ref, *, sub):
    d_y = y_ref.shape[1]
    for r in range(x_ref.shape[0] // sub):
        rows = slice(r * sub, (r + 1) * sub)
        h = _dot(y_ref[rows, :], w_ref[:d_y, :]) + _dot(o_ref[rows, :], w_ref[d_y:, :])
        out_ref[rows, :] = x_ref[rows, :] + _rms(h, g_ref[...])


def _hyb_outproj(y, o, xt, w_out, g):
    t, d = xt.shape
    sub = min(TOKEN_TILE, t)
    tm = min(OUTPROJ_SUBTILES * sub, t)
    d_y, d_o = y.shape[1], o.shape[1]
    row = lambda w: pl.BlockSpec((tm, w), lambda i: (i, 0))
    return pl.pallas_call(
        functools.partial(_outproj_body, sub=sub),
        grid=(t // tm,),
        in_specs=[row(d_y), row(d_o), row(d), _const_spec((d_y + d_o, d)), _const_spec((1, d))],
        out_specs=row(d),
        out_shape=jax.ShapeDtypeStruct((t, d), F32),
        compiler_params=_params("parallel"),
        name="hyb_outproj",
    )(y, o, xt, w_out.astype(BF16), g.reshape(1, d))


def _hybrid_layer(xt, b, s, g_pre, g_post, w_in, conv_w, conv_b, dt_bias, a_log, d_skip,
                  ssm_norm_g, fox_b_f, w_out):
    d_ssm = ssm_norm_g.shape[0]
    conv_dim = conv_w.shape[1]
    n_ssm_heads = dt_bias.shape[0]
    n_fox_heads = fox_b_f.shape[0]
    d_fox = n_fox_heads * FOX_HEAD_DIM
    z, xbc, q, k, dtf, vt = _hyb_inproj(xt, g_pre, w_in, d_ssm, conv_dim, n_ssm_heads,
                                        d_fox, n_fox_heads)
    r3 = lambda a: a.reshape(b, s, a.shape[-1])
    y, cum, cumt = _ssd(r3(xbc), r3(z), r3(dtf), conv_w, conv_b, dt_bias, a_log, d_skip,
                        ssm_norm_g, fox_b_f)
    o = _fox(r3(q), r3(k), vt, cum, cumt, n_ssm_heads)
    return _hyb_outproj(y.reshape(b * s, d_ssm), o.reshape(b * s, d_fox), xt, w_out, g_post)


def _rec_body(x_ref, gpre_ref, win_ref, convw_ref, convb_ref, wax_ref, ba_ref, bx_ref, lam_ref,
              wout_ref, gpost_ref, o_ref, buf_ref, a_ref, b_ref, h_ref, xn_ref, *, tt, sub):
    d_rnn = lam_ref.shape[1]
    n_blocks = d_rnn // RNN_BLOCK

    @pl.when(pl.program_id(1) == 0)
    def _():
        buf_ref[0:V7X_SUBLANES, :] = jnp.zeros((V7X_SUBLANES, d_rnn), F32)
        h_ref[...] = jnp.zeros(h_ref.shape, F32)

    log_sig_lam = -_softplus(-lam_ref[...])
    n_sub = tt // sub
    for r in range(n_sub):
        rows = slice(r * sub, (r + 1) * sub)
        x = x_ref[0, rows, :]
        xn = _rms(x, gpre_ref[...]).astype(BF16)
        xn_ref[rows, :] = xn
        xr = _dot(xn, win_ref[:, d_rnn:])

        xc = _causal_conv(buf_ref, xr, convw_ref, convb_ref, offset=r * sub, last=r == n_sub - 1)

        xcb = xc.astype(BF16)
        r_parts, i_parts = [], []
        for n in range(n_blocks):
            ri = _dot(xcb[:, n * RNN_BLOCK:(n + 1) * RNN_BLOCK], wax_ref[n])
            r_parts.append(ri[:, :RNN_BLOCK])
            i_parts.append(ri[:, RNN_BLOCK:])
        rg = _sigmoid(jnp.concatenate(r_parts, axis=1) + ba_ref[...])
        ig = _sigmoid(jnp.concatenate(i_parts, axis=1) + bx_ref[...])
        log_a = RG_LRU_C * rg * log_sig_lam
        a = jnp.exp(log_a)
        u = jnp.sqrt(jnp.tanh(-log_a) * (1.0 + a * a)) * (ig * xc)

        groups = (sub // V7X_SUBLANES, V7X_SUBLANES, d_rnn)
        a = a.reshape(groups)
        u = u.reshape(groups)
        rmod = lax.broadcasted_iota(jnp.int32, groups, 1)
        for sh in (1, 2, 4):
            m = rmod >= sh
            a_sh = jnp.where(m, pltpu.roll(a, sh, 1), 1.0)
            u_sh = jnp.where(m, pltpu.roll(u, sh, 1), 0.0)
            u = a * u_sh + u
            a = a * a_sh
        a_ref[rows, :] = a.reshape(sub, d_rnn)
        b_ref[rows, :] = u.reshape(sub, d_rnn)

    def group(gi, h):
        off = pl.multiple_of(gi * V7X_SUBLANES, V7X_SUBLANES)
        hr = a_ref[pl.ds(off, V7X_SUBLANES), :] * h + b_ref[pl.ds(off, V7X_SUBLANES), :]
        b_ref[pl.ds(off, V7X_SUBLANES), :] = hr
        return hr[V7X_SUBLANES - 1:V7X_SUBLANES, :]

    h_ref[...] = lax.fori_loop(0, tt // V7X_SUBLANES, group, h_ref[...], unroll=8)
    for r in range(n_sub):
        rows = slice(r * sub, (r + 1) * sub)
        gate = _gelu_tanh(_dot(xn_ref[rows, :], win_ref[:, :d_rnn]))
        y = (b_ref[rows, :] * gate).astype(BF16)
        o_ref[0, rows, :] = x_ref[0, rows, :] + _rms(_dot(y, wout_ref[...]), gpost_ref[...])


def _rec_layer(xt, b, s, g_pre, g_post, w_in, conv_w, conv_b, w_a, b_a, w_x, b_x, lam, w_out):
    d = xt.shape[1]
    d_rnn = lam.shape[0]
    sub = min(TOKEN_TILE, s)
    tt = min(REC_SUBTILES * sub, s)
    n_blocks = w_a.shape[0]
    w_ax = jnp.concatenate([w_a, w_x], axis=2).astype(BF16)
    row1 = lambda a: a.reshape(1, -1)
    blk = pl.BlockSpec((1, tt, d), lambda bi, ti: (bi, ti, 0))
    out = pl.pallas_call(
        functools.partial(_rec_body, tt=tt, sub=sub),
        grid=(b, s // tt),
        in_specs=[blk, _const_spec((1, d)), _const_spec((d, 2 * d_rnn)),
                  _const_spec((CONV_K, d_rnn)), _const_spec((1, d_rnn)),
                  _const_spec((n_blocks, RNN_BLOCK, 2 * RNN_BLOCK)),
                  _const_spec((1, d_rnn)), _const_spec((1, d_rnn)), _const_spec((1, d_rnn)),
                  _const_spec((d_rnn, d)), _const_spec((1, d))],
        out_specs=blk,
        out_shape=jax.ShapeDtypeStruct((b, s, d), F32),
        scratch_shapes=[pltpu.VMEM((tt + V7X_SUBLANES, d_rnn), F32),
                        pltpu.VMEM((tt, d_rnn), F32), pltpu.VMEM((tt, d_rnn), F32),
                        pltpu.VMEM((1, d_rnn), F32), pltpu.VMEM((tt, d), BF16)],
        compiler_params=_params("parallel", "arbitrary"),
        name="rec",
    )(xt.reshape(b, s, d), row1(g_pre), w_in.astype(BF16), conv_w, row1(conv_b), w_ax,
      row1(b_a), row1(b_x), row1(lam), w_out.astype(BF16), row1(g_post))
    return out.reshape(b * s, d)


def kernel(x, norm_g, ffn_w_in, ffn_w_out, hyb_w_in, ssm_conv_w, ssm_conv_b, ssm_dt_bias, ssm_a_log, ssm_d, ssm_norm_g, fox_b_f, hyb_w_out, rec_w_in, rec_conv_w, rec_conv_b, rec_w_a, rec_b_a, rec_w_x, rec_b_x, rec_lambda, rec_w_out):
    b, s, d = x.shape
    depth = norm_g.shape[0]
    xt = x.reshape(b * s, d)
    ffn_w_in = ffn_w_in.astype(BF16)
    ffn_w_out = ffn_w_out.astype(BF16)
    for layer in range(depth):
        g = norm_g[layer]
        xt = _ffn(xt, g[0], g[1], ffn_w_in[layer, 0], ffn_w_out[layer, 0])
        if layer % 2 == 0:
            i = layer // 2
            xt = _hybrid_layer(xt, b, s, g[2], g[3], hyb_w_in[i], ssm_conv_w[i], ssm_conv_b[i],
                               ssm_dt_bias[i], ssm_a_log[i], ssm_d[i], ssm_norm_g[i],
                               fox_b_f[i], hyb_w_out[i])
        else:
            j = layer // 2
            xt = _rec_layer(xt, b, s, g[2], g[3], rec_w_in[j], rec_conv_w[j], rec_conv_b[j],
                            rec_w_a[j], rec_b_a[j], rec_w_x[j], rec_b_x[j], rec_lambda[j],
                            rec_w_out[j])
        xt = _ffn(xt, g[4], g[5], ffn_w_in[layer, 1], ffn_w_out[layer, 1])
    return xt.reshape(b, s, d)
```

```python
import functools

import jax
import jax.numpy as jnp
from jax import lax
from jax.experimental import pallas as pl
from jax.experimental.pallas import tpu as pltpu

F32 = jnp.float32
BF16 = jnp.bfloat16

NORM_EPS = 1e-6
CONV_K = 4
SSM_HEAD_DIM = 64
SSM_GROUPS = 2
SSM_STATE = 128
SSD_CHUNK = 128
FOX_HEAD_DIM = 128
RG_LRU_C = 8.0
RNN_BLOCK = 128
LOG2E = 1.4426950408889634

V7X_LANES = 128
V7X_SUBLANES = 8
V7X_BF16_SUBLANES = 16
VMEM_LIMIT_BYTES = 56 * 1024 * 1024

TOKEN_TILE = 512
FFN_CHUNK = 256
FFN_SUBTILE_ROWS = 256
FFN_SUBTILES = 4
OUTPROJ_SUBTILES = 4
FOX_BLOCK = 512
REC_SUBTILES = 2


def _params(*semantics):
    return pltpu.CompilerParams(dimension_semantics=semantics,
                                vmem_limit_bytes=VMEM_LIMIT_BYTES)


def _const_spec(shape):
    zeros = (0,) * len(shape)
    return pl.BlockSpec(shape, lambda *_: zeros)


def _rms(x, g):
    return x * lax.rsqrt(jnp.mean(x * x, axis=-1, keepdims=True) + NORM_EPS) * g


def _sigmoid(x):
    return 0.5 * jnp.tanh(0.5 * x) + 0.5


def _silu(x):
    return x * _sigmoid(x)


def _gelu_tanh(x):
    inner = x * (0.7978845608028654 + 0.035677408136300125 * (x * x))
    return x * (0.5 * jnp.tanh(inner) + 0.5)


def _softplus(x):
    return jnp.maximum(x, 0.0) + jnp.log1p(jnp.exp(-jnp.abs(x)))


def _dot(a, b):
    return jnp.dot(a, b, preferred_element_type=F32)


def _split(x, terms):
    parts = []
    for _ in range(terms - 1):
        part = x.astype(BF16)
        parts.append(part)
        x = x - part.astype(F32)
    parts.append(x.astype(BF16))
    return parts


def _dot_exact_lhs(m, x, terms=3):
    return sum(_dot(m, part) for part in _split(x, terms))


def _dot_exact_rhs(x, m, terms=3):
    return sum(_dot(part, m) for part in _split(x, terms))


def _causal_conv(buf_ref, x, w_ref, b_ref, offset=0, last=True):
    rows = x.shape[0]
    base = V7X_SUBLANES + offset
    buf_ref[base:base + rows, :] = x
    w = w_ref[...]
    y = b_ref[...] + w[CONV_K - 1:CONV_K] * x
    for k in range(CONV_K - 1):
        y = y + w[k:k + 1] * buf_ref[pl.ds(base - (CONV_K - 1) + k, rows), :]
    if last:
        buf_ref[0:V7X_SUBLANES, :] = buf_ref[offset + rows:base + rows, :]
    return y


def _ffn_body(x_ref, gpre_ref, gpost_ref, win_ref, wout_ref, o_ref, h_ref, *, fc, n_chunks, sub):
    dff = fc * n_chunks
    for r in range(x_ref.shape[0] // sub):
        rows = slice(r * sub, (r + 1) * sub)
        x = x_ref[rows, :]
        xn = _rms(x, gpre_ref[...]).astype(BF16)
        for c in range(n_chunks):
            gate = _dot(xn, win_ref[:, fc * c:fc * (c + 1)])
            up = _dot(xn, win_ref[:, dff + fc * c:dff + fc * (c + 1)])
            h_ref[rows, fc * c:fc * (c + 1)] = (_silu(gate) * up).astype(BF16)
        y = _dot(h_ref[rows, :], wout_ref[...])
        o_ref[rows, :] = x + 0.5 * _rms(y, gpost_ref[...])


def _ffn(xt, g_pre, g_post, w_in, w_out):
    t, d = xt.shape
    dff = w_out.shape[0]
    fc = FFN_CHUNK
    n_chunks = dff // fc
    assert n_chunks * fc == dff
    sub = min(FFN_SUBTILE_ROWS, t)
    tm = min(FFN_SUBTILES * sub, t)
    resident = lambda shape: pl.BlockSpec(shape, lambda i: (0, 0), pipeline_mode=pl.Buffered(1))
    return pl.pallas_call(
        functools.partial(_ffn_body, fc=fc, n_chunks=n_chunks, sub=sub),
        grid=(t // tm,),
        in_specs=[pl.BlockSpec((tm, d), lambda i: (i, 0)),
                  _const_spec((1, d)), _const_spec((1, d)),
                  resident((d, 2 * dff)), resident((dff, d))],
        out_specs=pl.BlockSpec((tm, d), lambda i: (i, 0)),
        out_shape=jax.ShapeDtypeStruct((t, d), F32),
        scratch_shapes=[pltpu.VMEM((tm, dff), BF16)],
        compiler_params=_params("parallel"),
        name="ffn",
    )(xt, g_pre.reshape(1, d), g_post.reshape(1, d), w_in, w_out)


def _inproj_body(x_ref, g_ref, w_ref, wvt_ref, *o_refs, splits):
    xn = _rms(x_ref[...], g_ref[...]).astype(BF16)
    for o_ref, (start, width, mult) in zip(o_refs[:-1], splits):
        y = _dot(xn, w_ref[:, start:start + width])
        o_ref[...] = (y if mult is None else y * mult).astype(o_ref.dtype)
    vt = lax.dot_general(wvt_ref[...], xn, (((1,), (1,)), ((), ())), preferred_element_type=F32)
    o_refs[-1][...] = vt.astype(o_refs[-1].dtype)


def _hyb_inproj(xt, g, w_in, d_ssm, conv_dim, n_ssm_heads, d_fox, n_fox_heads):
    t, d = xt.shape
    tm = min(TOKEN_TILE, t)
    o_xbc = d_ssm
    o_dt = o_xbc + conv_dim
    o_q = o_dt + n_ssm_heads
    o_v = o_q + 2 * d_fox
    o_f = o_v + d_fox
    pad = V7X_LANES - n_ssm_heads - n_fox_heads
    w_perm = jnp.concatenate(
        [w_in[:, :o_dt], w_in[:, o_q:o_v], w_in[:, o_dt:o_q], w_in[:, o_f:],
         jnp.zeros((d, pad), w_in.dtype)], axis=1).astype(BF16)
    w_vt = w_in[:, o_v:o_f].T.astype(BF16)
    widths = (d_ssm, conv_dim, d_fox, d_fox, V7X_LANES)
    dtypes = (F32, F32, BF16, BF16, F32)
    mults = (None, None, FOX_HEAD_DIM ** -0.5 * LOG2E, None, None)
    splits, start = [], 0
    for w, mult in zip(widths, mults):
        splits.append((start, w, mult))
        start += w
    return pl.pallas_call(
        functools.partial(_inproj_body, splits=tuple(splits)),
        grid=(t // tm,),
        in_specs=[pl.BlockSpec((tm, d), lambda i: (i, 0)), _const_spec((1, d)),
                  _const_spec((d, start)), _const_spec((d_fox, d))],
        out_specs=[pl.BlockSpec((tm, w), lambda i: (i, 0)) for w in widths]
        + [pl.BlockSpec((d_fox, tm), lambda i: (0, i))],
        out_shape=[jax.ShapeDtypeStruct((t, w), dt) for w, dt in zip(widths, dtypes)]
        + [jax.ShapeDtypeStruct((d_fox, t), BF16)],
        compiler_params=_params("parallel"),
        name="hyb_inproj",
    )(xt, g.reshape(1, d), w_perm, w_vt)


def _ssd_body(xbc_ref, z_ref, dtf_ref, convw_ref, convb_ref, bias_ref, arow_ref, dskip_ref,
              ng_ref, tri_ref, expand_ref,
              y_ref, cum_ref, cumt_ref,
              buf_ref, state_ref, carry_ref, carryt_ref, *, n_heads, n_fox_heads):
    L = SSD_CHUNK
    hd = SSM_HEAD_DIM
    ns = SSM_STATE
    d_ssm = n_heads * hd
    gw = d_ssm // SSM_GROUPS
    heads_per_group = n_heads // SSM_GROUPS

    @pl.when(pl.program_id(1) == 0)
    def _():
        buf_ref[0:V7X_SUBLANES, :] = jnp.zeros((V7X_SUBLANES, buf_ref.shape[1]), F32)
        state_ref[...] = jnp.zeros(state_ref.shape, F32)
        carry_ref[...] = jnp.zeros(carry_ref.shape, F32)
        carryt_ref[...] = jnp.zeros(carryt_ref.shape, F32)

    act = _silu(_causal_conv(buf_ref, xbc_ref[0], convw_ref, convb_ref))
    xs = act[:, :d_ssm]
    bm = act[:, d_ssm:d_ssm + SSM_GROUPS * ns]
    cm = act[:, d_ssm + SSM_GROUPS * ns:]

    lane = lax.broadcasted_iota(jnp.int32, (L, V7X_LANES), 1)
    is_dt = lane < n_heads
    v = dtf_ref[0] + bias_ref[...]
    sp = _softplus(jnp.where(is_dt, v, -v))
    val = jnp.where(is_dt, sp, -sp)
    da = val * arow_ref[...]
    cs = _dot_exact_lhs(tri_ref[...], da) * LOG2E
    cst = cs.T

    expand = expand_ref[...]
    xdt = xs * _dot_exact_rhs(val, expand, terms=2)
    cs_x = _dot_exact_rhs(cs, expand)
    exp_a_x = jnp.exp2(cs_x)
    decay_x = jnp.exp2(cs_x[L - 1:L, :] - cs_x)

    row_i = lax.broadcasted_iota(jnp.int32, (L, L), 0)
    col_i = lax.broadcasted_iota(jnp.int32, (L, L), 1)
    causal = row_i >= col_i
    first_half = lane < hd

    y_parts = []
    for g in range(SSM_GROUPS):
        bg = bm[:, g * ns:(g + 1) * ns]
        cg = cm[:, g * ns:(g + 1) * ns].astype(BF16)
        cb = lax.dot_general(cg, bg.astype(BF16), (((1,), (1,)), ((), ())),
                             preferred_element_type=F32)
        gs = slice(g * gw, (g + 1) * gw)
        s_prev = state_ref[g]
        y_off = _dot(cg, s_prev.astype(BF16)) * exp_a_x[:, gs]
        contrib = _dot(bg.T.astype(BF16), (xdt[:, gs] * decay_x[:, gs]).astype(BF16))
        state_ref[g] = s_prev * exp_a_x[L - 1:L, gs] + contrib
        for pr in range(heads_per_group // 2):
            c0 = g * gw + pr * 2 * hd
            x_pair = xdt[:, c0:c0 + 2 * hd]
            acc = None
            for j in range(2):
                h = g * heads_per_group + 2 * pr + j
                seg = cs[:, h:h + 1] - cst[h:h + 1, :]
                lmat = jnp.exp2(jnp.where(causal, seg, -jnp.inf))
                m = (lmat * cb).astype(BF16)
                keep = first_half if j == 0 else jnp.logical_not(first_half)
                part = _dot(m, jnp.where(keep, x_pair, 0.0).astype(BF16))
                acc = part if acc is None else acc + part
            y_parts.append(acc + y_off[:, c0 - g * gw:c0 - g * gw + 2 * hd])
    y = jnp.concatenate(y_parts, axis=1) + dskip_ref[...] * xs
    y = y * _silu(z_ref[0])
    normed = []
    for g in range(SSM_GROUPS):
        yg = y[:, g * gw:(g + 1) * gw]
        normed.append(yg * lax.rsqrt(jnp.mean(yg * yg, axis=-1, keepdims=True) + NORM_EPS))
    y_ref[0] = (jnp.concatenate(normed, axis=1) * ng_ref[...]).astype(y_ref.dtype)

    cum = cs + carry_ref[...]
    cum_ref[0] = cum
    carry_ref[...] = cum[L - 1:L, :]
    cumt = cst + carryt_ref[...]
    cumt_ref[0] = cumt[n_heads:n_heads + n_fox_heads, :]
    carryt_ref[...] = jnp.broadcast_to(cumt[:, L - 1:L], carryt_ref.shape)


def _ssd(xbc, z, dtf, conv_w, conv_b, dt_bias, a_log, d_skip, norm_g, fox_b_f):
    b, s, conv_dim = xbc.shape
    d_ssm = z.shape[-1]
    n_heads = dt_bias.shape[0]
    n_fox = fox_b_f.shape[0]
    L = SSD_CHUNK
    pad = V7X_LANES - n_heads - n_fox
    bias_row = jnp.concatenate([dt_bias, fox_b_f, jnp.zeros((pad,), F32)]).reshape(1, V7X_LANES)
    a_row = jnp.concatenate([-jnp.exp(a_log), jnp.ones((n_fox,), F32),
                             jnp.zeros((pad,), F32)]).reshape(1, V7X_LANES)
    d_row = jnp.repeat(d_skip, SSM_HEAD_DIM).reshape(1, d_ssm)
    tri = (jnp.arange(L)[:, None] >= jnp.arange(L)[None, :]).astype(BF16)
    expand = (jnp.arange(V7X_LANES)[:, None] == (jnp.arange(d_ssm)[None, :] // SSM_HEAD_DIM)).astype(BF16)
    blk = lambda w: pl.BlockSpec((1, L, w), lambda i, c: (i, c, 0))
    return pl.pallas_call(
        functools.partial(_ssd_body, n_heads=n_heads, n_fox_heads=n_fox),
        grid=(b, s // L),
        in_specs=[blk(conv_dim), blk(d_ssm), blk(V7X_LANES),
                  _const_spec((CONV_K, conv_dim)), _const_spec((1, conv_dim)),
                  _const_spec((1, V7X_LANES)), _const_spec((1, V7X_LANES)),
                  _const_spec((1, d_ssm)), _const_spec((1, d_ssm)),
                  _const_spec((L, L)), _const_spec((V7X_LANES, d_ssm))],
        out_specs=[blk(d_ssm), blk(V7X_LANES),
                   pl.BlockSpec((1, n_fox, L), lambda i, c: (i, 0, c))],
        out_shape=[jax.ShapeDtypeStruct((b, s, d_ssm), BF16),
                   jax.ShapeDtypeStruct((b, s, V7X_LANES), F32),
                   jax.ShapeDtypeStruct((b, n_fox, s), F32)],
        scratch_shapes=[pltpu.VMEM((L + V7X_SUBLANES, conv_dim), F32),
                        pltpu.VMEM((SSM_GROUPS, SSM_STATE, d_ssm // SSM_GROUPS), F32),
                        pltpu.VMEM((1, V7X_LANES), F32),
                        pltpu.VMEM((V7X_LANES, L), F32)],
        compiler_params=_params("parallel", "arbitrary"),
        name="ssd",
    )(xbc, z, dtf, conv_w, conv_b.reshape(1, conv_dim), bias_row, a_row, d_row,
      norm_g.reshape(1, d_ssm), tri, expand)


def _fox_body(q_ref, k_ref, vt_ref, cum_ref, cumt_ref, o_ref, s_ref, bm_ref, m_ref, l_ref, acc_ref, *,
              n_heads, blk, cum_col0):
    i = pl.program_id(1)
    hd = FOX_HEAD_DIM
    half = n_heads // 2
    groups = (range(half), range(half, n_heads))
    key_i = lax.broadcasted_iota(jnp.int32, (blk, blk), 0)
    qry_i = lax.broadcasted_iota(jnp.int32, (blk, blk), 1)
    causal = key_i <= qry_i
    m_ref[...] = jnp.full(m_ref.shape, -jnp.inf, F32)
    l_ref[...] = jnp.zeros(l_ref.shape, F32)
    acc_ref[...] = jnp.zeros(acc_ref.shape, F32)
    cq2 = cumt_ref[0]
    ones_rows = jnp.ones((V7X_BF16_SUBLANES, blk), BF16)

    def scores(g, j, masked=False):
        start = pl.multiple_of(j * blk, blk)
        for h in groups[g]:
            hs = slice(h * hd, (h + 1) * hd)
            k = k_ref[0, pl.ds(start, blk), hs]
            ck2 = cum_ref[0, pl.ds(start, blk), cum_col0 + h:cum_col0 + h + 1]
            s = lax.dot_general(k, q_ref[0, :, hs], (((1,), (1,)), ((), ())),
                                preferred_element_type=F32) - ck2
            if masked:
                s = jnp.where(causal, s, -jnp.inf)
            s_ref[h] = s
            bm_ref[h, 0:1, :] = jnp.max(s, axis=0, keepdims=True)

    def softmax_pv(g, j):
        start = pl.multiple_of(j * blk, blk)
        for h in groups[g]:
            cq2h = cq2[h:h + 1]
            m_old = m_ref[h, 0:1, :]
            m_new = jnp.maximum(m_old, cq2h + bm_ref[h, 0:1, :])
            alpha = jnp.exp2(m_old - m_new)
            p = jnp.exp2(s_ref[h] - (m_new - cq2h)).astype(BF16)
            m_ref[h, 0:1, :] = m_new
            vt = jnp.concatenate([vt_ref[h * hd:(h + 1) * hd, pl.ds(start, blk)], ones_rows], axis=0)
            pv = _dot(vt, p)
            acc_ref[h] = alpha * acc_ref[h] + pv[:hd]
            l_ref[h, 0:1, :] = alpha * l_ref[h, 0:1, :] + pv[hd:hd + 1]

    scores(0, i, masked=True)
    scores(1, i, masked=True)
    softmax_pv(0, i)

    def step(j, carry):
        scores(0, j)
        softmax_pv(1, jnp.where(j == 0, i, j - 1))
        scores(1, j)
        softmax_pv(0, j)
        return carry

    lax.fori_loop(0, i, step, 0)
    softmax_pv(1, jnp.where(i == 0, i, i - 1))

    for h in range(n_heads):
        o = acc_ref[h] * (1.0 / l_ref[h, 0:1, :])
        o_ref[0, :, h * hd:(h + 1) * hd] = o.T.astype(o_ref.dtype)


def _fox(q, k, vt, cum, cumt, n_ssm_heads):
    b, s, d_fox = q.shape
    n_heads = cumt.shape[1]
    blk = min(FOX_BLOCK, s)
    assert s % blk == 0 and n_heads % 2 == 0
    qspec = pl.BlockSpec((1, blk, d_fox), lambda bi, i: (bi, i, 0))
    return pl.pallas_call(
        functools.partial(_fox_body, n_heads=n_heads, blk=blk, cum_col0=n_ssm_heads),
        grid=(b, s // blk),
        in_specs=[qspec,
                  pl.BlockSpec((1, s, d_fox), lambda bi, i: (bi, 0, 0)),
                  pl.BlockSpec((d_fox, s), lambda bi, i: (0, bi)),
                  pl.BlockSpec((1, s, V7X_LANES), lambda bi, i: (bi, 0, 0)),
                  pl.BlockSpec((1, n_heads, blk), lambda bi, i: (bi, 0, i))],
        out_specs=qspec,
        out_shape=jax.ShapeDtypeStruct((b, s, d_fox), BF16),
        scratch_shapes=[pltpu.VMEM((n_heads, blk, blk), F32),
                        pltpu.VMEM((n_heads, V7X_SUBLANES, blk), F32),
                        pltpu.VMEM((n_heads, V7X_SUBLANES, blk), F32),
                        pltpu.VMEM((n_heads, V7X_SUBLANES, blk), F32),
                        pltpu.VMEM((n_heads, FOX_HEAD_DIM, blk), F32)],
        compiler_params=_params("parallel", "arbitrary"),
        name="fox",
    )(q, k, vt, cum, cumt)


def _outproj_body(y_ref, o_ref, x_ref, w_ref, g_ref, out_ref, *, sub):
    d_y = y_ref.shape[1]
    for r in range(x_ref.shape[0] // sub):
        rows = slice(r * sub, (r + 1) * sub)
        h = _dot(y_ref[rows, :], w_ref[:d_y, :]) + _dot(o_ref[rows, :], w_ref[d_y:, :])
        out_ref[rows, :] = x_ref[rows, :] + _rms(h, g_ref[...])


def _hyb_outproj(y, o, xt, w_out, g):
    t, d = xt.shape
    sub = min(TOKEN_TILE, t)
    tm = min(OUTPROJ_SUBTILES * sub, t)
    d_y, d_o = y.shape[1], o.shape[1]
    row = lambda w: pl.BlockSpec((tm, w), lambda i: (i, 0))
    return pl.pallas_call(
        functools.partial(_outproj_body, sub=sub),
        grid=(t // tm,),
        in_specs=[row(d_y), row(d_o), row(d), _const_spec((d_y + d_o, d)), _const_spec((1, d))],
        out_specs=row(d),
        out_shape=jax.ShapeDtypeStruct((t, d), F32),
        compiler_params=_params("parallel"),
        name="hyb_outproj",
    )(y, o, xt, w_out.astype(BF16), g.reshape(1, d))


def _hybrid_layer(xt, b, s, g_pre, g_post, w_in, conv_w, conv_b, dt_bias, a_log, d_skip,
                  ssm_norm_g, fox_b_f, w_out):
    d_ssm = ssm_norm_g.shape[0]
    conv_dim = conv_w.shape[1]
    n_ssm_heads = dt_bias.shape[0]
    n_fox_heads = fox_b_f.shape[0]
    d_fox = n_fox_heads * FOX_HEAD_DIM
    z, xbc, q, k, dtf, vt = _hyb_inproj(xt, g_pre, w_in, d_ssm, conv_dim, n_ssm_heads,
                                        d_fox, n_fox_heads)
    r3 = lambda a: a.reshape(b, s, a.shape[-1])
    y, cum, cumt = _ssd(r3(xbc), r3(z), r3(dtf), conv_w, conv_b, dt_bias, a_log, d_skip,
                        ssm_norm_g, fox_b_f)
    o = _fox(r3(q), r3(k), vt, cum, cumt, n_ssm_heads)
    return _hyb_outproj(y.reshape(b * s, d_ssm), o.reshape(b * s, d_fox), xt, w_out, g_post)


def _rec_body(x_ref, gpre_ref, win_ref, convw_ref, convb_ref, wax_ref, ba_ref, bx_ref, lam_ref,
              wout_ref, gpost_ref, o_ref, buf_ref, a_ref, b_ref, h_ref, xn_ref, *, tt, sub):
    d_rnn = lam_ref.shape[1]
    n_blocks = d_rnn // RNN_BLOCK

    @pl.when(pl.program_id(1) == 0)
    def _():
        buf_ref[0:V7X_SUBLANES, :] = jnp.zeros((V7X_SUBLANES, d_rnn), F32)
        h_ref[...] = jnp.zeros(h_ref.shape, F32)

    log_sig_lam = -_softplus(-lam_ref[...])
    n_sub = tt // sub
    for r in range(n_sub):
        rows = slice(r * sub, (r + 1) * sub)
        x = x_ref[0, rows, :]
        xn = _rms(x, gpre_ref[...]).astype(BF16)
        xn_ref[rows, :] = xn
        xr = _dot(xn, win_ref[:, d_rnn:])

        xc = _causal_conv(buf_ref, xr, convw_ref, convb_ref, offset=r * sub, last=r == n_sub - 1)

        xcb = xc.astype(BF16)
        r_parts, i_parts = [], []
        for n in range(n_blocks):
            ri = _dot(xcb[:, n * RNN_BLOCK:(n + 1) * RNN_BLOCK], wax_ref[n])
            r_parts.append(ri[:, :RNN_BLOCK])
            i_parts.append(ri[:, RNN_BLOCK:])
        rg = _sigmoid(jnp.concatenate(r_parts, axis=1) + ba_ref[...])
        ig = _sigmoid(jnp.concatenate(i_parts, axis=1) + bx_ref[...])
        log_a = RG_LRU_C * rg * log_sig_lam
        a = jnp.exp(log_a)
        u = jnp.sqrt(jnp.tanh(-log_a) * (1.0 + a * a)) * (ig * xc)

        groups = (sub // V7X_SUBLANES, V7X_SUBLANES, d_rnn)
        a = a.reshape(groups)
        u = u.reshape(groups)
        rmod = lax.broadcasted_iota(jnp.int32, groups, 1)
        for sh in (1, 2, 4):
            m = rmod >= sh
            a_sh = jnp.where(m, pltpu.roll(a, sh, 1), 1.0)
            u_sh = jnp.where(m, pltpu.roll(u, sh, 1), 0.0)
            u = a * u_sh + u
            a = a * a_sh
        a_ref[rows, :] = a.reshape(sub, d_rnn)
        b_ref[rows, :] = u.reshape(sub, d_rnn)

    def group(gi, h):
        off = pl.multiple_of(gi * V7X_SUBLANES, V7X_SUBLANES)
        hr = a_ref[pl.ds(off, V7X_SUBLANES), :] * h + b_ref[pl.ds(off, V7X_SUBLANES), :]
        b_ref[pl.ds(off, V7X_SUBLANES), :] = hr
        return hr[V7X_SUBLANES - 1:V7X_SUBLANES, :]

    h_ref[...] = lax.fori_loop(0, tt // V7X_SUBLANES, group, h_ref[...], unroll=8)
    for r in range(n_sub):
        rows = slice(r * sub, (r + 1) * sub)
        gate = _gelu_tanh(_dot(xn_ref[rows, :], win_ref[:, :d_rnn]))
        y = (b_ref[rows, :] * gate).astype(BF16)
        o_ref[0, rows, :] = x_ref[0, rows, :] + _rms(_dot(y, wout_ref[...]), gpost_ref[...])


def _rec_layer(xt, b, s, g_pre, g_post, w_in, conv_w, conv_b, w_a, b_a, w_x, b_x, lam, w_out):
    d = xt.shape[1]
    d_rnn = lam.shape[0]
    sub = min(TOKEN_TILE, s)
    tt = min(REC_SUBTILES * sub, s)
    n_blocks = w_a.shape[0]
    w_ax = jnp.concatenate([w_a, w_x], axis=2).astype(BF16)
    row1 = lambda a: a.reshape(1, -1)
    blk = pl.BlockSpec((1, tt, d), lambda bi, ti: (bi, ti, 0))
    out = pl.pallas_call(
        functools.partial(_rec_body, tt=tt, sub=sub),
        grid=(b, s // tt),
        in_specs=[blk, _const_spec((1, d)), _const_spec((d, 2 * d_rnn)),
                  _const_spec((CONV_K, d_rnn)), _const_spec((1, d_rnn)),
                  _const_spec((n_blocks, RNN_BLOCK, 2 * RNN_BLOCK)),
                  _const_spec((1, d_rnn)), _const_spec((1, d_rnn)), _const_spec((1, d_rnn)),
                  _const_spec((d_rnn, d)), _const_spec((1, d))],
        out_specs=blk,
        out_shape=jax.ShapeDtypeStruct((b, s, d), F32),
        scratch_shapes=[pltpu.VMEM((tt + V7X_SUBLANES, d_rnn), F32),
                        pltpu.VMEM((tt, d_rnn), F32), pltpu.VMEM((tt, d_rnn), F32),
                        pltpu.VMEM((1, d_rnn), F32), pltpu.VMEM((tt, d), BF16)],
        compiler_params=_params("parallel", "arbitrary"),
        name="rec",
    )(xt.reshape(b, s, d), row1(g_pre), w_in.astype(BF16), conv_w, row1(conv_b), w_ax,
      row1(b_a), row1(b_x), row1(lam), w_out.astype(BF16), row1(g_post))
    return out.reshape(b * s, d)


def kernel(x, norm_g, ffn_w_in, ffn_w_out, hyb_w_in, ssm_conv_w, ssm_conv_b, ssm_dt_bias, ssm_a_log, ssm_d, ssm_norm_g, fox_b_f, hyb_w_out, rec_w_in, rec_conv_w, rec_conv_b, rec_w_a, rec_b_a, rec_w_x, rec_b_x, rec_lambda, rec_w_out):
    b, s, d = x.shape
    depth = norm_g.shape[0]
    xt = x.reshape(b * s, d)
    ffn_w_in = ffn_w_in.astype(BF16)
    ffn_w_out = ffn_w_out.astype(BF16)
    for layer in range(depth):
        g = norm_g[layer]
        xt = _ffn(xt, g[0], g[1], ffn_w_in[layer, 0], ffn_w_out[layer, 0])
        if layer % 2 == 0:
            i = layer // 2
            xt = _hybrid_layer(xt, b, s, g[2], g[3], hyb_w_in[i], ssm_conv_w[i], ssm_conv_b[i],
                               ssm_dt_bias[i], ssm_a_log[i], ssm_d[i], ssm_norm_g[i],
                               fox_b_f[i], hyb_w_out[i])
        else:
            j = layer // 2
            xt = _rec_layer(xt, b, s, g[2], g[3], rec_w_in[j], rec_conv_w[j], rec_conv_b[j],
                            rec_w_a[j], rec_b_a[j], rec_w_x[j], rec_b_x[j], rec_lambda[j],
                            rec_w_out[j])
        xt = _ffn(xt, g[4], g[5], ffn_w_in[layer, 1], ffn_w_out[layer, 1])
    return xt.reshape(b, s, d)
```

```python
import functools

import jax
import jax.numpy as jnp
from jax import lax
from jax.experimental import pallas as pl
from jax.experimental.pallas import tpu as pltpu

F32 = jnp.float32
BF16 = jnp.bfloat16

NORM_EPS = 1e-6
CONV_K = 4
SSM_HEAD_DIM = 64
SSM_GROUPS = 2
SSM_STATE = 128
SSD_CHUNK = 128
FOX_HEAD_DIM = 128
RG_LRU_C = 8.0
RNN_BLOCK = 128
LOG2E = 1.4426950408889634

V7X_LANES = 128
V7X_SUBLANES = 8
V7X_BF16_SUBLANES = 16
VMEM_LIMIT_BYTES = 56 * 1024 * 1024

TOKEN_TILE = 512
FFN_CHUNK = 256
FFN_SUBTILE_ROWS = 256
FFN_SUBTILES = 4
OUTPROJ_SUBTILES = 4
FOX_BLOCK = 512
SSD_SUBTILES = 4
REC_SUBTILES = 2


def _params(*semantics):
    return pltpu.CompilerParams(dimension_semantics=semantics,
                                vmem_limit_bytes=VMEM_LIMIT_BYTES)


def _const_spec(shape):
    zeros = (0,) * len(shape)
    return pl.BlockSpec(shape, lambda *_: zeros)


def _rms(x, g):
    return x * lax.rsqrt(jnp.mean(x * x, axis=-1, keepdims=True) + NORM_EPS) * g


def _sigmoid(x):
    return 0.5 * jnp.tanh(0.5 * x) + 0.5


def _silu(x):
    return x * _sigmoid(x)


def _gelu_tanh(x):
    inner = x * (0.7978845608028654 + 0.035677408136300125 * (x * x))
    return x * (0.5 * jnp.tanh(inner) + 0.5)


def _softplus(x):
    return jnp.maximum(x, 0.0) + jnp.log1p(jnp.exp(-jnp.abs(x)))


def _dot(a, b):
    return jnp.dot(a, b, preferred_element_type=F32)


def _split(x, terms):
    parts = []
    for _ in range(terms - 1):
        part = x.astype(BF16)
        parts.append(part)
        x = x - part.astype(F32)
    parts.append(x.astype(BF16))
    return parts


def _dot_exact_lhs(m, x, terms=3):
    return sum(_dot(m, part) for part in _split(x, terms))


def _dot_exact_rhs(x, m, terms=3):
    return sum(_dot(part, m) for part in _split(x, terms))


def _causal_conv(buf_ref, x, w_ref, b_ref, offset=0, last=True):
    rows = x.shape[0]
    base = V7X_SUBLANES + offset
    buf_ref[base:base + rows, :] = x
    w = w_ref[...]
    y = b_ref[...] + w[CONV_K - 1:CONV_K] * x
    for k in range(CONV_K - 1):
        y = y + w[k:k + 1] * buf_ref[pl.ds(base - (CONV_K - 1) + k, rows), :]
    if last:
        buf_ref[0:V7X_SUBLANES, :] = buf_ref[offset + rows:base + rows, :]
    return y


def _ffn_body(x_ref, gpre_ref, gpost_ref, win_ref, wout_ref, o_ref, h_ref, *, fc, n_chunks, sub):
    dff = fc * n_chunks
    for r in range(x_ref.shape[0] // sub):
        rows = slice(r * sub, (r + 1) * sub)
        x = x_ref[rows, :]
        xn = _rms(x, gpre_ref[...]).astype(BF16)
        for c in range(n_chunks):
            gate = _dot(xn, win_ref[:, fc * c:fc * (c + 1)])
            up = _dot(xn, win_ref[:, dff + fc * c:dff + fc * (c + 1)])
            h_ref[rows, fc * c:fc * (c + 1)] = (_silu(gate) * up).astype(BF16)
        y = _dot(h_ref[rows, :], wout_ref[...])
        o_ref[rows, :] = x + 0.5 * _rms(y, gpost_ref[...])


def _ffn(xt, g_pre, g_post, w_in, w_out):
    t, d = xt.shape
    dff = w_out.shape[0]
    fc = FFN_CHUNK
    n_chunks = dff // fc
    assert n_chunks * fc == dff
    sub = min(FFN_SUBTILE_ROWS, t)
    tm = min(FFN_SUBTILES * sub, t)
    resident = lambda shape: pl.BlockSpec(shape, lambda i: (0, 0), pipeline_mode=pl.Buffered(1))
    return pl.pallas_call(
        functools.partial(_ffn_body, fc=fc, n_chunks=n_chunks, sub=sub),
        grid=(t // tm,),
        in_specs=[pl.BlockSpec((tm, d), lambda i: (i, 0)),
                  _const_spec((1, d)), _const_spec((1, d)),
                  resident((d, 2 * dff)), resident((dff, d))],
        out_specs=pl.BlockSpec((tm, d), lambda i: (i, 0)),
        out_shape=jax.ShapeDtypeStruct((t, d), F32),
        scratch_shapes=[pltpu.VMEM((tm, dff), BF16)],
        compiler_params=_params("parallel"),
        name="ffn",
    )(xt, g_pre.reshape(1, d), g_post.reshape(1, d), w_in, w_out)


def _inproj_body(x_ref, g_ref, w_ref, wvt_ref, *o_refs, splits):
    xn = _rms(x_ref[...], g_ref[...]).astype(BF16)
    for o_ref, (start, width, mult) in zip(o_refs[:-1], splits):
        y = _dot(xn, w_ref[:, start:start + width])
        o_ref[...] = (y if mult is None else y * mult).astype(o_ref.dtype)
    vt = lax.dot_general(wvt_ref[...], xn, (((1,), (1,)), ((), ())), preferred_element_type=F32)
    o_refs[-1][...] = vt.astype(o_refs[-1].dtype)


def _hyb_inproj(xt, g, w_in, d_ssm, conv_dim, n_ssm_heads, d_fox, n_fox_heads):
    t, d = xt.shape
    tm = min(TOKEN_TILE, t)
    o_xbc = d_ssm
    o_dt = o_xbc + conv_dim
    o_q = o_dt + n_ssm_heads
    o_v = o_q + 2 * d_fox
    o_f = o_v + d_fox
    pad = V7X_LANES - n_ssm_heads - n_fox_heads
    w_perm = jnp.concatenate(
        [w_in[:, :o_dt], w_in[:, o_q:o_v], w_in[:, o_dt:o_q], w_in[:, o_f:],
         jnp.zeros((d, pad), w_in.dtype)], axis=1).astype(BF16)
    w_vt = w_in[:, o_v:o_f].T.astype(BF16)
    widths = (d_ssm, conv_dim, d_fox, d_fox, V7X_LANES)
    dtypes = (F32, F32, BF16, BF16, F32)
    mults = (None, None, FOX_HEAD_DIM ** -0.5 * LOG2E, None, None)
    splits, start = [], 0
    for w, mult in zip(widths, mults):
        splits.append((start, w, mult))
        start += w
    return pl.pallas_call(
        functools.partial(_inproj_body, splits=tuple(splits)),
        grid=(t // tm,),
        in_specs=[pl.BlockSpec((tm, d), lambda i: (i, 0)), _const_spec((1, d)),
                  _const_spec((d, start)), _const_spec((d_fox, d))],
        out_specs=[pl.BlockSpec((tm, w), lambda i: (i, 0)) for w in widths]
        + [pl.BlockSpec((d_fox, tm), lambda i: (0, i))],
        out_shape=[jax.ShapeDtypeStruct((t, w), dt) for w, dt in zip(widths, dtypes)]
        + [jax.ShapeDtypeStruct((d_fox, t), BF16)],
        compiler_params=_params("parallel"),
        name="hyb_inproj",
    )(xt, g.reshape(1, d), w_perm, w_vt)


def _ssd_body(xbc_ref, z_ref, dtf_ref, convw_ref, convb_ref, bias_ref, arow_ref, dskip_ref,
              ng_ref, tri_ref, expand_ref,
              y_ref, cum_ref, cumt_ref,
              buf_ref, state_ref, carry_ref, carryt_ref, *, n_heads, n_fox_heads):
    L = SSD_CHUNK
    hd = SSM_HEAD_DIM
    ns = SSM_STATE
    d_ssm = n_heads * hd
    gw = d_ssm // SSM_GROUPS
    heads_per_group = n_heads // SSM_GROUPS

    @pl.when(pl.program_id(1) == 0)
    def _():
        buf_ref[0:V7X_SUBLANES, :] = jnp.zeros((V7X_SUBLANES, buf_ref.shape[1]), F32)
        state_ref[...] = jnp.zeros(state_ref.shape, F32)
        carry_ref[...] = jnp.zeros(carry_ref.shape, F32)
        carryt_ref[...] = jnp.zeros(carryt_ref.shape, F32)

    n_sub = xbc_ref.shape[1] // L
    for c in range(n_sub):
        _ssd_chunk(c, n_sub, xbc_ref, z_ref, dtf_ref, convw_ref, convb_ref, bias_ref, arow_ref,
                   dskip_ref, ng_ref, tri_ref, expand_ref, y_ref, cum_ref, cumt_ref,
                   buf_ref, state_ref, carry_ref, carryt_ref, n_heads=n_heads, n_fox_heads=n_fox_heads)


def _ssd_chunk(c, n_sub, xbc_ref, z_ref, dtf_ref, convw_ref, convb_ref, bias_ref, arow_ref,
               dskip_ref, ng_ref, tri_ref, expand_ref, y_ref, cum_ref, cumt_ref,
               buf_ref, state_ref, carry_ref, carryt_ref, *, n_heads, n_fox_heads):
    L = SSD_CHUNK
    hd = SSM_HEAD_DIM
    ns = SSM_STATE
    d_ssm = n_heads * hd
    gw = d_ssm // SSM_GROUPS
    heads_per_group = n_heads // SSM_GROUPS
    rows = slice(c * L, (c + 1) * L)

    act = _silu(_causal_conv(buf_ref, xbc_ref[0, rows, :], convw_ref, convb_ref, offset=c * L,
                             last=c == n_sub - 1))
    xs = act[:, :d_ssm]
    bm = act[:, d_ssm:d_ssm + SSM_GROUPS * ns]
    cm = act[:, d_ssm + SSM_GROUPS * ns:]

    lane = lax.broadcasted_iota(jnp.int32, (L, V7X_LANES), 1)
    is_dt = lane < n_heads
    v = dtf_ref[0, rows, :] + bias_ref[...]
    sp = _softplus(jnp.where(is_dt, v, -v))
    val = jnp.where(is_dt, sp, -sp)
    da = val * arow_ref[...]
    cs = _dot_exact_lhs(tri_ref[...], da) * LOG2E
    cst = cs.T

    expand = expand_ref[...]
    xdt = xs * _dot_exact_rhs(val, expand, terms=2)
    cs_x = _dot_exact_rhs(cs, expand)
    exp_a_x = jnp.exp2(cs_x)
    decay_x = jnp.exp2(cs_x[L - 1:L, :] - cs_x)

    row_i = lax.broadcasted_iota(jnp.int32, (L, L), 0)
    col_i = lax.broadcasted_iota(jnp.int32, (L, L), 1)
    causal = row_i >= col_i
    first_half = lane < hd

    y_parts = []
    for g in range(SSM_GROUPS):
        bg = bm[:, g * ns:(g + 1) * ns]
        cg = cm[:, g * ns:(g + 1) * ns].astype(BF16)
        cb = lax.dot_general(cg, bg.astype(BF16), (((1,), (1,)), ((), ())),
                             preferred_element_type=F32)
        gs = slice(g * gw, (g + 1) * gw)
        s_prev = state_ref[g]
        y_off = _dot(cg, s_prev.astype(BF16)) * exp_a_x[:, gs]
        contrib = _dot(bg.T.astype(BF16), (xdt[:, gs] * decay_x[:, gs]).astype(BF16))
        state_ref[g] = s_prev * exp_a_x[L - 1:L, gs] + contrib
        for pr in range(heads_per_group // 2):
            c0 = g * gw + pr * 2 * hd
            x_pair = xdt[:, c0:c0 + 2 * hd]
            acc = None
            for j in range(2):
                h = g * heads_per_group + 2 * pr + j
                seg = cs[:, h:h + 1] - cst[h:h + 1, :]
                lmat = jnp.exp2(jnp.where(causal, seg, -jnp.inf))
                m = (lmat * cb).astype(BF16)
                keep = first_half if j == 0 else jnp.logical_not(first_half)
                part = _dot(m, jnp.where(keep, x_pair, 0.0).astype(BF16))
                acc = part if acc is None else acc + part
            y_parts.append(acc + y_off[:, c0 - g * gw:c0 - g * gw + 2 * hd])
    y = jnp.concatenate(y_parts, axis=1) + dskip_ref[...] * xs
    y = y * _silu(z_ref[0, rows, :])
    normed = []
    for g in range(SSM_GROUPS):
        yg = y[:, g * gw:(g + 1) * gw]
        normed.append(yg * lax.rsqrt(jnp.mean(yg * yg, axis=-1, keepdims=True) + NORM_EPS))
    y_ref[0, rows, :] = (jnp.concatenate(normed, axis=1) * ng_ref[...]).astype(y_ref.dtype)

    cum = cs + carry_ref[...]
    cum_ref[0, rows, :] = cum
    carry_ref[...] = cum[L - 1:L, :]
    cumt = cst + carryt_ref[...]
    cumt_ref[0, :, rows] = cumt[n_heads:n_heads + n_fox_heads, :]
    carryt_ref[...] = jnp.broadcast_to(cumt[:, L - 1:L], carryt_ref.shape)


def _ssd(xbc, z, dtf, conv_w, conv_b, dt_bias, a_log, d_skip, norm_g, fox_b_f):
    b, s, conv_dim = xbc.shape
    d_ssm = z.shape[-1]
    n_heads = dt_bias.shape[0]
    n_fox = fox_b_f.shape[0]
    L = SSD_CHUNK
    pad = V7X_LANES - n_heads - n_fox
    bias_row = jnp.concatenate([dt_bias, fox_b_f, jnp.zeros((pad,), F32)]).reshape(1, V7X_LANES)
    a_row = jnp.concatenate([-jnp.exp(a_log), jnp.ones((n_fox,), F32),
                             jnp.zeros((pad,), F32)]).reshape(1, V7X_LANES)
    d_row = jnp.repeat(d_skip, SSM_HEAD_DIM).reshape(1, d_ssm)
    tri = (jnp.arange(L)[:, None] >= jnp.arange(L)[None, :]).astype(BF16)
    expand = (jnp.arange(V7X_LANES)[:, None] == (jnp.arange(d_ssm)[None, :] // SSM_HEAD_DIM)).astype(BF16)
    rows = min(SSD_SUBTILES * L, s)
    blk = lambda w: pl.BlockSpec((1, rows, w), lambda i, c: (i, c, 0))
    return pl.pallas_call(
        functools.partial(_ssd_body, n_heads=n_heads, n_fox_heads=n_fox),
        grid=(b, s // rows),
        in_specs=[blk(conv_dim), blk(d_ssm), blk(V7X_LANES),
                  _const_spec((CONV_K, conv_dim)), _const_spec((1, conv_dim)),
                  _const_spec((1, V7X_LANES)), _const_spec((1, V7X_LANES)),
                  _const_spec((1, d_ssm)), _const_spec((1, d_ssm)),
                  _const_spec((L, L)), _const_spec((V7X_LANES, d_ssm))],
        out_specs=[blk(d_ssm), blk(V7X_LANES),
                   pl.BlockSpec((1, n_fox, rows), lambda i, c: (i, 0, c))],
        out_shape=[jax.ShapeDtypeStruct((b, s, d_ssm), BF16),
                   jax.ShapeDtypeStruct((b, s, V7X_LANES), F32),
                   jax.ShapeDtypeStruct((b, n_fox, s), F32)],
        scratch_shapes=[pltpu.VMEM((rows + V7X_SUBLANES, conv_dim), F32),
                        pltpu.VMEM((SSM_GROUPS, SSM_STATE, d_ssm // SSM_GROUPS), F32),
                        pltpu.VMEM((1, V7X_LANES), F32),
                        pltpu.VMEM((V7X_LANES, L), F32)],
        compiler_params=_params("parallel", "arbitrary"),
        name="ssd",
    )(xbc, z, dtf, conv_w, conv_b.reshape(1, conv_dim), bias_row, a_row, d_row,
      norm_g.reshape(1, d_ssm), tri, expand)


def _fox_body(q_ref, k_ref, vt_ref, cum_ref, cumt_ref, o_ref, s_ref, bm_ref, m_ref, l_ref, acc_ref, *,
              n_heads, blk, cum_col0):
    i = pl.program_id(1)
    hd = FOX_HEAD_DIM
    half = n_heads // 2
    groups = (range(half), range(half, n_heads))
    key_i = lax.broadcasted_iota(jnp.int32, (blk, blk), 0)
    qry_i = lax.broadcasted_iota(jnp.int32, (blk, blk), 1)
    causal = key_i <= qry_i
    m_ref[...] = jnp.full(m_ref.shape, -jnp.inf, F32)
    l_ref[...] = jnp.zeros(l_ref.shape, F32)
    acc_ref[...] = jnp.zeros(acc_ref.shape, F32)
    cq2 = cumt_ref[0]
    ones_rows = jnp.ones((V7X_BF16_SUBLANES, blk), BF16)

    def scores(g, j, masked=False):
        start = pl.multiple_of(j * blk, blk)
        for h in groups[g]:
            hs = slice(h * hd, (h + 1) * hd)
            k = k_ref[0, pl.ds(start, blk), hs]
            ck2 = cum_ref[0, pl.ds(start, blk), cum_col0 + h:cum_col0 + h + 1]
            s = lax.dot_general(k, q_ref[0, :, hs], (((1,), (1,)), ((), ())),
                                preferred_element_type=F32) - ck2
            if masked:
                s = jnp.where(causal, s, -jnp.inf)
            s_ref[h] = s
            bm_ref[h, 0:1, :] = jnp.max(s, axis=0, keepdims=True)

    def softmax_pv(g, j):
        start = pl.multiple_of(j * blk, blk)
        for h in groups[g]:
            cq2h = cq2[h:h + 1]
            m_old = m_ref[h, 0:1, :]
            m_new = jnp.maximum(m_old, cq2h + bm_ref[h, 0:1, :])
            alpha = jnp.exp2(m_old - m_new)
            p = jnp.exp2(s_ref[h] - (m_new - cq2h)).astype(BF16)
            m_ref[h, 0:1, :] = m_new
            vt = jnp.concatenate([vt_ref[h * hd:(h + 1) * hd, pl.ds(start, blk)], ones_rows], axis=0)
            pv = _dot(vt, p)
            acc_ref[h] = alpha * acc_ref[h] + pv[:hd]
            l_ref[h, 0:1, :] = alpha * l_ref[h, 0:1, :] + pv[hd:hd + 1]

    scores(0, i, masked=True)
    scores(1, i, masked=True)
    softmax_pv(0, i)

    def step(j, carry):
        scores(0, j)
        softmax_pv(1, jnp.where(j == 0, i, j - 1))
        scores(1, j)
        softmax_pv(0, j)
        return carry

    lax.fori_loop(0, i, step, 0)
    softmax_pv(1, jnp.where(i == 0, i, i - 1))

    for h in range(n_heads):
        o = acc_ref[h] * (1.0 / l_ref[h, 0:1, :])
        o_ref[0, :, h * hd:(h + 1) * hd] = o.T.astype(o_ref.dtype)


def _fox(q, k, vt, cum, cumt, n_ssm_heads):
    b, s, d_fox = q.shape
    n_heads = cumt.shape[1]
    blk = min(FOX_BLOCK, s)
    assert s % blk == 0 and n_heads % 2 == 0
    qspec = pl.BlockSpec((1, blk, d_fox), lambda bi, i: (bi, i, 0))
    return pl.pallas_call(
        functools.partial(_fox_body, n_heads=n_heads, blk=blk, cum_col0=n_ssm_heads),
        grid=(b, s // blk),
        in_specs=[qspec,
                  pl.BlockSpec((1, s, d_fox), lambda bi, i: (bi, 0, 0)),
                  pl.BlockSpec((d_fox, s), lambda bi, i: (0, bi)),
                  pl.BlockSpec((1, s, V7X_LANES), lambda bi, i: (bi, 0, 0)),
                  pl.BlockSpec((1, n_heads, blk), lambda bi, i: (bi, 0, i))],
        out_specs=qspec,
        out_shape=jax.ShapeDtypeStruct((b, s, d_fox), BF16),
        scratch_shapes=[pltpu.VMEM((n_heads, blk, blk), F32),
                        pltpu.VMEM((n_heads, V7X_SUBLANES, blk), F32),
                        pltpu.VMEM((n_heads, V7X_SUBLANES, blk), F32),
                        pltpu.VMEM((n_heads, V7X_SUBLANES, blk), F32),
                        pltpu.VMEM((n_heads, FOX_HEAD_DIM, blk), F32)],
        compiler_params=_params("parallel", "arbitrary"),
        name="fox",
    )(q, k, vt, cum, cumt)


def _outproj_body(y_ref, o_ref, x_ref, w_ref, g_ref, out_ref, *, sub):
    d_y = y_ref.shape[1]
    for r in range(x_ref.shape[0] // sub):
        rows = slice(r * sub, (r + 1) * sub)
        h = _dot(y_ref[rows, :], w_ref[:d_y, :]) + _dot(o_ref[rows, :], w_ref[d_y:, :])
        out_ref[rows, :] = x_ref[rows, :] + _rms(h, g_ref[...])


def _hyb_outproj(y, o, xt, w_out, g):
    t, d = xt.shape
    sub = min(TOKEN_TILE, t)
    tm = min(OUTPROJ_SUBTILES * sub, t)
    d_y, d_o = y.shape[1], o.shape[1]
    row = lambda w: pl.BlockSpec((tm, w), lambda i: (i, 0))
    return pl.pallas_call(
        functools.partial(_outproj_body, sub=sub),
        grid=(t // tm,),
        in_specs=[row(d_y), row(d_o), row(d), _const_spec((d_y + d_o, d)), _const_spec((1, d))],
        out_specs=row(d),
        out_shape=jax.ShapeDtypeStruct((t, d), F32),
        compiler_params=_params("parallel"),
        name="hyb_outproj",
    )(y, o, xt, w_out.astype(BF16), g.reshape(1, d))


def _hybrid_layer(xt, b, s, g_pre, g_post, w_in, conv_w, conv_b, dt_bias, a_log, d_skip,
                  ssm_norm_g, fox_b_f, w_out):
    d_ssm = ssm_norm_g.shape[0]
    conv_dim = conv_w.shape[1]
    n_ssm_heads = dt_bias.shape[0]
    n_fox_heads = fox_b_f.shape[0]
    d_fox = n_fox_heads * FOX_HEAD_DIM
    z, xbc, q, k, dtf, vt = _hyb_inproj(xt, g_pre, w_in, d_ssm, conv_dim, n_ssm_heads,
                                        d_fox, n_fox_heads)
    r3 = lambda a: a.reshape(b, s, a.shape[-1])
    y, cum, cumt = _ssd(r3(xbc), r3(z), r3(dtf), conv_w, conv_b, dt_bias, a_log, d_skip,
                        ssm_norm_g, fox_b_f)
    o = _fox(r3(q), r3(k), vt, cum, cumt, n_ssm_heads)
    return _hyb_outproj(y.reshape(b * s, d_ssm), o.reshape(b * s, d_fox), xt, w_out, g_post)


def _rec_body(x_ref, gpre_ref, win_ref, convw_ref, convb_ref, wax_ref, ba_ref, bx_ref, lam_ref,
              wout_ref, gpost_ref, o_ref, buf_ref, a_ref, b_ref, h_ref, xn_ref, *, tt, sub):
    d_rnn = lam_ref.shape[1]
    n_blocks = d_rnn // RNN_BLOCK

    @pl.when(pl.program_id(1) == 0)
    def _():
        buf_ref[0:V7X_SUBLANES, :] = jnp.zeros((V7X_SUBLANES, d_rnn), F32)
        h_ref[...] = jnp.zeros(h_ref.shape, F32)

    log_sig_lam = -_softplus(-lam_ref[...])
    n_sub = tt // sub
    for r in range(n_sub):
        rows = slice(r * sub, (r + 1) * sub)
        x = x_ref[0, rows, :]
        xn = _rms(x, gpre_ref[...]).astype(BF16)
        xn_ref[rows, :] = xn
        xr = _dot(xn, win_ref[:, d_rnn:])

        xc = _causal_conv(buf_ref, xr, convw_ref, convb_ref, offset=r * sub, last=r == n_sub - 1)

        xcb = xc.astype(BF16)
        r_parts, i_parts = [], []
        for n in range(n_blocks):
            ri = _dot(xcb[:, n * RNN_BLOCK:(n + 1) * RNN_BLOCK], wax_ref[n])
            r_parts.append(ri[:, :RNN_BLOCK])
            i_parts.append(ri[:, RNN_BLOCK:])
        rg = _sigmoid(jnp.concatenate(r_parts, axis=1) + ba_ref[...])
        ig = _sigmoid(jnp.concatenate(i_parts, axis=1) + bx_ref[...])
        log_a = RG_LRU_C * rg * log_sig_lam
        a = jnp.exp(log_a)
        u = jnp.sqrt(jnp.tanh(-log_a) * (1.0 + a * a)) * (ig * xc)

        groups = (sub // V7X_SUBLANES, V7X_SUBLANES, d_rnn)
        a = a.reshape(groups)
        u = u.reshape(groups)
        rmod = lax.broadcasted_iota(jnp.int32, groups, 1)
        for sh in (1, 2, 4):
            m = rmod >= sh
            a_sh = jnp.where(m, pltpu.roll(a, sh, 1), 1.0)
            u_sh = jnp.where(m, pltpu.roll(u, sh, 1), 0.0)
            u = a * u_sh + u
            a = a * a_sh
        a_ref[rows, :] = a.reshape(sub, d_rnn)
        b_ref[rows, :] = u.reshape(sub, d_rnn)

    def group(gi, h):
        off = pl.multiple_of(gi * V7X_SUBLANES, V7X_SUBLANES)
        hr = a_ref[pl.ds(off, V7X_SUBLANES), :] * h + b_ref[pl.ds(off, V7X_SUBLANES), :]
        b_ref[pl.ds(off, V7X_SUBLANES), :] = hr
        return hr[V7X_SUBLANES - 1:V7X_SUBLANES, :]

    h_ref[...] = lax.fori_loop(0, tt // V7X_SUBLANES, group, h_ref[...], unroll=8)
    for r in range(n_sub):
        rows = slice(r * sub, (r + 1) * sub)
        gate = _gelu_tanh(_dot(xn_ref[rows, :], win_ref[:, :d_rnn]))
        y = (b_ref[rows, :] * gate).astype(BF16)
        o_ref[0, rows, :] = x_ref[0, rows, :] + _rms(_dot(y, wout_ref[...]), gpost_ref[...])


def _rec_layer(xt, b, s, g_pre, g_post, w_in, conv_w, conv_b, w_a, b_a, w_x, b_x, lam, w_out):
    d = xt.shape[1]
    d_rnn = lam.shape[0]
    sub = min(TOKEN_TILE, s)
    tt = min(REC_SUBTILES * sub, s)
    n_blocks = w_a.shape[0]
    w_ax = jnp.concatenate([w_a, w_x], axis=2).astype(BF16)
    row1 = lambda a: a.reshape(1, -1)
    blk = pl.BlockSpec((1, tt, d), lambda bi, ti: (bi, ti, 0))
    out = pl.pallas_call(
        functools.partial(_rec_body, tt=tt, sub=sub),
        grid=(b, s // tt),
        in_specs=[blk, _const_spec((1, d)), _const_spec((d, 2 * d_rnn)),
                  _const_spec((CONV_K, d_rnn)), _const_spec((1, d_rnn)),
                  _const_spec((n_blocks, RNN_BLOCK, 2 * RNN_BLOCK)),
                  _const_spec((1, d_rnn)), _const_spec((1, d_rnn)), _const_spec((1, d_rnn)),
                  _const_spec((d_rnn, d)), _const_spec((1, d))],
        out_specs=blk,
        out_shape=jax.ShapeDtypeStruct((b, s, d), F32),
        scratch_shapes=[pltpu.VMEM((tt + V7X_SUBLANES, d_rnn), F32),
                        pltpu.VMEM((tt, d_rnn), F32), pltpu.VMEM((tt, d_rnn), F32),
                        pltpu.VMEM((1, d_rnn), F32), pltpu.VMEM((tt, d), BF16)],
        compiler_params=_params("parallel", "arbitrary"),
        name="rec",
    )(xt.reshape(b, s, d), row1(g_pre), w_in.astype(BF16), conv_w, row1(conv_b), w_ax,
      row1(b_a), row1(b_x), row1(lam), w_out.astype(BF16), row1(g_post))
    return out.reshape(b * s, d)


def kernel(x, norm_g, ffn_w_in, ffn_w_out, hyb_w_in, ssm_conv_w, ssm_conv_b, ssm_dt_bias, ssm_a_log, ssm_d, ssm_norm_g, fox_b_f, hyb_w_out, rec_w_in, rec_conv_w, rec_conv_b, rec_w_a, rec_b_a, rec_w_x, rec_b_x, rec_lambda, rec_w_out):
    b, s, d = x.shape
    depth = norm_g.shape[0]
    xt = x.reshape(b * s, d)
    ffn_w_in = ffn_w_in.astype(BF16)
    ffn_w_out = ffn_w_out.astype(BF16)
    for layer in range(depth):
        g = norm_g[layer]
        xt = _ffn(xt, g[0], g[1], ffn_w_in[layer, 0], ffn_w_out[layer, 0])
        if layer % 2 == 0:
            i = layer // 2
            xt = _hybrid_layer(xt, b, s, g[2], g[3], hyb_w_in[i], ssm_conv_w[i], ssm_conv_b[i],
                               ssm_dt_bias[i], ssm_a_log[i], ssm_d[i], ssm_norm_g[i],
                               fox_b_f[i], hyb_w_out[i])
        else:
            j = layer // 2
            xt = _rec_layer(xt, b, s, g[2], g[3], rec_w_in[j], rec_conv_w[j], rec_conv_b[j],
                            rec_w_a[j], rec_b_a[j], rec_w_x[j], rec_b_x[j], rec_lambda[j],
                            rec_w_out[j])
        xt = _ffn(xt, g[4], g[5], ffn_w_in[layer, 1], ffn_w_out[layer, 1])
    return xt.reshape(b, s, d)
```
